```python
import math
import jax, jax.numpy as jnp
from jax import lax
import numpy as np

D_MODEL = 1024
BATCH = 16
SEQ = 2048
DEPTH = 1
DEC_BATCH = 128
DEC_SEQ = 8
PAST_LEN = 16384
PAGE_SIZE = 128

N_HEADS = 16
N_KV_HEADS = 4
HEAD_DIM = 64
KV_GROUP = N_HEADS // N_KV_HEADS
ATTN_WIDTH = N_HEADS * HEAD_DIM
KV_WIDTH = N_KV_HEADS * HEAD_DIM
WINDOW = 128
W_BUF = min(WINDOW, PAST_LEN)
ROPE_DIM = HEAD_DIM // 4
ROPE_THETA = 500000.0
SSD_WIDTH = 2 * D_MODEL
SSD_HEAD_DIM = 64
SSD_HEADS = SSD_WIDTH // SSD_HEAD_DIM
SSD_GROUPS = 4
SSD_HPG = SSD_HEADS // SSD_GROUPS
D_STATE = 128
CONV_WIDTH = 4
CONV_DIM = SSD_WIDTH + 2 * SSD_GROUPS * D_STATE
SSD_CHUNK = 128
N_BRANCH = 2
EPS = 1e-6
OFF_Q = ATTN_WIDTH
OFF_K = OFF_Q + KV_WIDTH
OFF_V = OFF_K + KV_WIDTH
OFF_ZA = OFF_V + ATTN_WIDTH
OFF_ZS = OFF_ZA + SSD_WIDTH
OFF_XBC = OFF_ZS + CONV_DIM
OFF_DT = OFF_XBC + SSD_HEADS
IN_DIM = OFF_DT + N_BRANCH * D_MODEL

kernel_name = "hybrid_swa_sink_ssd_gated_step"


def rms_norm(x, w):
    xf = x.astype(jnp.float32)
    y = xf * lax.rsqrt(jnp.mean(xf * xf, -1, keepdims=True) + EPS)
    return (y * w.astype(jnp.float32)).astype(x.dtype)


def partial_rope(x, pos):
    half = ROPE_DIM // 2
    inv_freq = jnp.power(ROPE_THETA, -jnp.arange(half, dtype=jnp.float32) * (2.0 / ROPE_DIM))
    ang = pos.astype(jnp.float32)[:, None] * inv_freq[None, :]
    cos = jnp.cos(ang)[None, :, None, :]
    sin = jnp.sin(ang)[None, :, None, :]
    xf = x.astype(jnp.float32)
    x1, x2 = xf[..., :half], xf[..., half:ROPE_DIM]
    out = jnp.concatenate([x1 * cos - x2 * sin, x2 * cos + x1 * sin, xf[..., ROPE_DIM:]], -1)
    return out.astype(x.dtype)


def sink_attention(q, k, v, mask, sinks):
    s = jnp.einsum('bnqkgd,bnskd->bnkgqs', q, k).astype(jnp.float32) * (HEAD_DIM ** -0.5)
    s = jnp.where(mask[None, :, None, None], s, -jnp.inf)
    sk = sinks.astype(jnp.float32).reshape(N_KV_HEADS, KV_GROUP)[:, :, None, None]
    m = jnp.maximum(s.max(-1, keepdims=True), sk)
    p = jnp.exp(s - m)
    p = p / (p.sum(-1, keepdims=True) + jnp.exp(sk - m))
    return jnp.einsum('bnkgqs,bnskd->bnqkgd', p.astype(v.dtype), v)


def window_attention_prompt(q, k, v, sinks):
    b, L = q.shape[:2]
    nb = L // WINDOW
    qb = q.reshape(b, nb, WINDOW, N_KV_HEADS, KV_GROUP, HEAD_DIM)

    def band(t):
        tb = t.reshape(b, nb, WINDOW, N_KV_HEADS, HEAD_DIM)
        prev = jnp.pad(tb, ((0, 0), (1, 0), (0, 0), (0, 0), (0, 0)))[:, :-1]
        return jnp.concatenate([prev, tb], axis=2)

    blk = jnp.arange(nb)[:, None] * WINDOW
    qpos = blk + jnp.arange(WINDOW)[None]
    kpos = blk - WINDOW + jnp.arange(2 * WINDOW)[None]
    d = qpos[:, :, None] - kpos[:, None, :]
    mask = (d >= 0) & (d < WINDOW) & (kpos[:, None, :] >= 0)
    o = sink_attention(qb, band(k), band(v), mask, sinks)
    return o.reshape(b, L, ATTN_WIDTH)


def window_attention_decode(q, k_new, v_new, cache_k, cache_v, sinks):
    b, T = q.shape[:2]
    nbuf = cache_k.shape[1]
    k_all = jnp.concatenate([cache_k.astype(k_new.dtype), k_new], 1)
    v_all = jnp.concatenate([cache_v.astype(v_new.dtype), v_new], 1)
    qpos = PAST_LEN + jnp.arange(T)
    kpos = PAST_LEN - nbuf + jnp.arange(nbuf + T)
    d = qpos[:, None] - kpos[None, :]
    mask = (d >= 0) & (d < WINDOW)
    o = sink_attention(q.reshape(b, 1, T, N_KV_HEADS, KV_GROUP, HEAD_DIM),
                       k_all[:, None], v_all[:, None], mask[None], sinks)
    return o.reshape(b, T, ATTN_WIDTH), k_all[:, -nbuf:], v_all[:, -nbuf:]


def causal_conv(xbc, conv_state, w, bias):
    xp = jnp.concatenate([conv_state.astype(xbc.dtype), xbc], 1)
    y = lax.conv_general_dilated(xp, w[:, None, :].astype(xp.dtype), (1,), 'VALID',
                                 dimension_numbers=('NWC', 'WIO', 'NWC'),
                                 feature_group_count=CONV_DIM)
    return jax.nn.silu(y + bias.astype(y.dtype)), xp[:, -(CONV_WIDTH - 1):]


def ssd_scan(x, dt, A, Bm, Cm, h0, chunk):
    b, L = x.shape[:2]
    c = L // chunk
    xs = x.reshape(b, c, chunk, SSD_GROUPS, SSD_HPG, SSD_HEAD_DIM)
    dts = dt.reshape(b, c, chunk, SSD_GROUPS, SSD_HPG)
    Bs = Bm.reshape(b, c, chunk, SSD_GROUPS, D_STATE)
    Cs = Cm.reshape(b, c, chunk, SSD_GROUPS, D_STATE)
    a_cs = jnp.cumsum(dts * A.reshape(SSD_GROUPS, SSD_HPG), axis=2)
    seg = a_cs[:, :, :, None] - a_cs[:, :, None]
    causal = jnp.tril(jnp.ones((chunk, chunk), dtype=bool))
    Lm = jnp.exp(jnp.where(causal[:, :, None, None], seg, -jnp.inf))
    cb = jnp.einsum('bclgn,bcsgn->bclsg', Cs, Bs)
    dx = dts[..., None] * xs
    y_diag = jnp.einsum('bclsg,bclsgh,bcsghp->bclghp', cb, Lm, dx)
    decay_to_end = jnp.exp(a_cs[:, :, -1:] - a_cs)
    states = jnp.einsum('bclgn,bclgh,bclghp->bcghpn', Bs, decay_to_end * dts, xs)
    chunk_decay = jnp.exp(a_cs[:, :, -1])

    def step(h, inp):
        st, dec = inp
        return h * dec[..., None, None] + st, h

    h0g = h0.reshape(b, SSD_GROUPS, SSD_HPG, SSD_HEAD_DIM, D_STATE)
    h_last, h_prev = lax.scan(step, h0g, (jnp.moveaxis(states, 1, 0), jnp.moveaxis(chunk_decay, 1, 0)))
    h_prev = jnp.moveaxis(h_prev, 0, 1)
    y_off = jnp.einsum('bclgn,bcghpn,bclgh->bclghp', Cs, h_prev, jnp.exp(a_cs))
    y = (y_diag + y_off).reshape(b, L, SSD_HEADS, SSD_HEAD_DIM)
    return y, h_last.reshape(b, SSD_HEADS, SSD_HEAD_DIM, D_STATE)


def ssd_branch(xbc_raw, z_s, dt_raw, conv_state, ssm_state, p, chunk):
    b, L = xbc_raw.shape[:2]
    xbc, new_conv = causal_conv(xbc_raw, conv_state, p['conv_w'], p['conv_b'])
    xs, Bm, Cm = jnp.split(xbc.astype(jnp.float32), [SSD_WIDTH, SSD_WIDTH + SSD_GROUPS * D_STATE], axis=-1)
    dt = jax.nn.softplus(dt_raw.astype(jnp.float32) + p['dt_bias'].astype(jnp.float32))
    A = -jnp.exp(p['A_log'].astype(jnp.float32))
    xh = xs.reshape(b, L, SSD_HEADS, SSD_HEAD_DIM)
    y, h_last = ssd_scan(xh, dt, A, Bm.reshape(b, L, SSD_GROUPS, D_STATE),
                         Cm.reshape(b, L, SSD_GROUPS, D_STATE), ssm_state.astype(jnp.float32), chunk)
    y = y + p['D_skip'].astype(jnp.float32)[:, None] * xh
    y = y.reshape(b, L, SSD_WIDTH) * jax.nn.silu(z_s.astype(jnp.float32))
    yg = y.reshape(b, L, SSD_GROUPS, SSD_WIDTH // SSD_GROUPS)
    yg = yg * lax.rsqrt(jnp.mean(yg * yg, -1, keepdims=True) + EPS)
    y = yg.reshape(b, L, SSD_WIDTH) * p['ssd_norm_w'].astype(jnp.float32)
    return y.astype(xbc_raw.dtype), new_conv, h_last.astype(ssm_state.dtype)


def layer_forward(x, pos, cache_k, cache_v, conv_state, ssm_state, p, decode):
    b, L, _ = x.shape
    h = rms_norm(x, p['norm_w'])
    proj = h @ p['w_in'].astype(h.dtype)
    q, k, v, z_a, z_s, xbc, dt_raw, g = jnp.split(
        proj, [OFF_Q, OFF_K, OFF_V, OFF_ZA, OFF_ZS, OFF_XBC, OFF_DT], axis=-1)
    q = partial_rope(rms_norm(q.reshape(b, L, N_HEADS, HEAD_DIM), p['q_norm_w']), pos)
    k = partial_rope(rms_norm(k.reshape(b, L, N_KV_HEADS, HEAD_DIM), p['k_norm_w']), pos)
    v = v.reshape(b, L, N_KV_HEADS, HEAD_DIM)
    if decode:
        o_a, new_k, new_v = window_attention_decode(q, k, v, cache_k, cache_v, p['sinks'])
        chunk = L
    else:
        o_a = window_attention_prompt(q, k, v, p['sinks'])
        nkeep = min(WINDOW, L)
        new_k, new_v = k[:, -nkeep:], v[:, -nkeep:]
        chunk = SSD_CHUNK
    o_s, new_conv, new_ssm = ssd_branch(xbc, z_s, dt_raw, conv_state, ssm_state, p, chunk)
    p_a = (o_a * jax.nn.silu(z_a)) @ p['w_attn_proj'].astype(o_a.dtype)
    p_s = o_s @ p['w_ssd_proj'].astype(o_s.dtype)
    g_a, g_s = jnp.split(g, 2, axis=-1)
    merged = jax.nn.sigmoid(g_a) * p_a + jax.nn.sigmoid(g_s) * p_s
    y = x + merged @ p['w_out'].astype(merged.dtype)
    return y, new_k, new_v, new_conv, new_ssm


def setup_inputs(seed: int = 0) -> dict:
    key = jax.random.key(seed)
    ks = jax.random.split(key, 24)
    f32 = jnp.float32
    nrm = lambda k, s, sc: jax.random.normal(k, s, f32) * sc
    dt0 = jnp.exp(jax.random.uniform(ks[0], (DEPTH, SSD_HEADS), f32, math.log(1e-3), math.log(1e-1)))
    return {
        'x_prompt': nrm(ks[1], (BATCH, SEQ, D_MODEL), 1.0),
        'x_sample': nrm(ks[2], (DEC_BATCH, DEC_SEQ, D_MODEL), 1.0),
        'cache_k': nrm(ks[3], (DEPTH, DEC_BATCH, W_BUF, N_KV_HEADS, HEAD_DIM), 1.0),
        'cache_v': nrm(ks[4], (DEPTH, DEC_BATCH, W_BUF, N_KV_HEADS, HEAD_DIM), 1.0),
        'state_conv': nrm(ks[5], (DEPTH, DEC_BATCH, CONV_WIDTH - 1, CONV_DIM), 1.0),
        'state_ssm': nrm(ks[6], (DEPTH, DEC_BATCH, SSD_HEADS, SSD_HEAD_DIM, D_STATE), 0.1),
        'norm_w': 1.0 + nrm(ks[7], (DEPTH, D_MODEL), 0.02),
        'w_in': nrm(ks[8], (DEPTH, D_MODEL, IN_DIM), D_MODEL ** -0.5),
        'q_norm_w': 1.0 + nrm(ks[9], (DEPTH, HEAD_DIM), 0.02),
        'k_norm_w': 1.0 + nrm(ks[10], (DEPTH, HEAD_DIM), 0.02),
        'sinks': nrm(ks[11], (DEPTH, N_HEADS), 1.0),
        'conv_w': nrm(ks[12], (DEPTH, CONV_WIDTH, CONV_DIM), CONV_WIDTH ** -0.5),
        'conv_b': nrm(ks[13], (DEPTH, CONV_DIM), 0.01),
        'dt_bias': dt0 + jnp.log(-jnp.expm1(-dt0)),
        'A_log': jnp.log(jax.random.uniform(ks[14], (DEPTH, SSD_HEADS), f32, 1.0, 16.0)),
        'D_skip': 1.0 + nrm(ks[15], (DEPTH, SSD_HEADS), 0.02),
        'ssd_norm_w': 1.0 + nrm(ks[16], (DEPTH, SSD_WIDTH), 0.02),
        'w_attn_proj': nrm(ks[17], (DEPTH, ATTN_WIDTH, D_MODEL), ATTN_WIDTH ** -0.5),
        'w_ssd_proj': nrm(ks[18], (DEPTH, SSD_WIDTH, D_MODEL), SSD_WIDTH ** -0.5),
        'w_out': nrm(ks[19], (DEPTH, D_MODEL, D_MODEL), D_MODEL ** -0.5),
    }


def reference(x_prompt, x_sample, cache_k, cache_v, state_conv, state_ssm,
              norm_w, w_in, q_norm_w, k_norm_w, sinks, conv_w, conv_b, dt_bias,
              A_log, D_skip, ssd_norm_w, w_attn_proj, w_ssd_proj, w_out):
    pos_p = jnp.arange(x_prompt.shape[1], dtype=jnp.int32)
    pos_s = PAST_LEN + jnp.arange(x_sample.shape[1], dtype=jnp.int32)
    bp = x_prompt.shape[0]
    yp, ys = x_prompt, x_sample
    pk, pv, pc, pss, sk_, sv, sc, sss = [], [], [], [], [], [], [], []
    for l in range(DEPTH):
        p = dict(norm_w=norm_w[l], w_in=w_in[l], q_norm_w=q_norm_w[l], k_norm_w=k_norm_w[l],
                 sinks=sinks[l], conv_w=conv_w[l], conv_b=conv_b[l], dt_bias=dt_bias[l],
                 A_log=A_log[l], D_skip=D_skip[l], ssd_norm_w=ssd_norm_w[l],
                 w_attn_proj=w_attn_proj[l], w_ssd_proj=w_ssd_proj[l], w_out=w_out[l])
        zc = jnp.zeros((bp, CONV_WIDTH - 1, CONV_DIM), x_prompt.dtype)
        zs = jnp.zeros((bp, SSD_HEADS, SSD_HEAD_DIM, D_STATE), state_ssm.dtype)
        yp, k1, v1, c1, s1 = layer_forward(yp, pos_p, None, None, zc, zs, p, False)
        ys, k2, v2, c2, s2 = layer_forward(ys, pos_s, cache_k[l], cache_v[l], state_conv[l], state_ssm[l], p, True)
        pk.append(k1); pv.append(v1); pc.append(c1); pss.append(s1)
        sk_.append(k2); sv.append(v2); sc.append(c2); sss.append(s2)
    return (yp, ys, jnp.stack(pk), jnp.stack(pv), jnp.stack(pc), jnp.stack(pss),
            jnp.stack(sk_), jnp.stack(sv), jnp.stack(sc), jnp.stack(sss))
```

```python
import functools

import jax
import jax.numpy as jnp
from jax import lax
from jax.experimental import pallas as pl
from jax.experimental.pallas import tpu as pltpu

F32 = jnp.float32
BF16 = jnp.bfloat16

D_MODEL = 1024
N_HEADS = 16
N_KV_HEADS = 4
HEAD_DIM = 64
KV_GROUP = N_HEADS // N_KV_HEADS
ATTN_WIDTH = N_HEADS * HEAD_DIM
KV_WIDTH = N_KV_HEADS * HEAD_DIM
WINDOW = 128
ROPE_DIM = HEAD_DIM // 4
ROPE_THETA = 500000.0
PAST_LEN = 16384
SSD_WIDTH = 2 * D_MODEL
SSD_HEAD_DIM = 64
SSD_HEADS = SSD_WIDTH // SSD_HEAD_DIM
SSD_GROUPS = 4
SSD_HPG = SSD_HEADS // SSD_GROUPS
D_STATE = 128
CONV_WIDTH = 4
CONV_DIM = SSD_WIDTH + 2 * SSD_GROUPS * D_STATE
SSD_CHUNK = 128
EPS = 1e-6

OFF_Q = ATTN_WIDTH
OFF_K = OFF_Q + KV_WIDTH
OFF_V = OFF_K + KV_WIDTH
OFF_ZA = OFF_V + ATTN_WIDTH
OFF_ZS = OFF_ZA + SSD_WIDTH
OFF_XBC = OFF_ZS + CONV_DIM
OFF_DT = OFF_XBC + SSD_HEADS

LANES = 128
SUBLANES = 8
VMEM_LIMIT_BYTES = 56 * 1024 * 1024

COL_ZS = 0
COL_G = COL_ZS + SSD_WIDTH
COL_Q = COL_G + 2 * D_MODEL
COL_ZA = COL_Q + ATTN_WIDTH
COL_XBC = COL_ZA + ATTN_WIDTH
COL_KV = COL_XBC + CONV_DIM
N_MAIN = COL_KV + 2 * KV_WIDTH
PROJ_TILE_N = 512
GROUP_WIDTH = SSD_WIDTH // SSD_GROUPS
HALF = LANES // 2
NEG_BIG = -1e30


def _sigmoid(x):
    return 1.0 / (1.0 + jnp.exp(-x))


def _silu(x):
    return x * _sigmoid(x)


def _split2(v):
    hi = v.astype(BF16)
    lo = (v - hi.astype(F32)).astype(BF16)
    return hi, lo


def _inproj_body(x_ref, nw_ref, w_ref, wdt_ref, out_ref, dt_ref, h_scr):
    x = x_ref[...]
    ms = jnp.mean(x * x, axis=-1, keepdims=True)
    h_scr[...] = (x * lax.rsqrt(ms + EPS) * nw_ref[...]).astype(BF16)
    for j in range(N_MAIN // PROJ_TILE_N):
        sl = slice(j * PROJ_TILE_N, (j + 1) * PROJ_TILE_N)
        out_ref[:, sl] = jnp.dot(h_scr[...], w_ref[:, sl], preferred_element_type=F32).astype(out_ref.dtype)
    dt_ref[...] = jnp.dot(h_scr[...], wdt_ref[...], preferred_element_type=F32)


def _inproj(x2d, norm_w, w_main, w_dt, tm, out_dtype):
    m = x2d.shape[0]
    return pl.pallas_call(
        _inproj_body,
        grid=(m // tm,),
        in_specs=[
            pl.BlockSpec((tm, D_MODEL), lambda i: (i, 0)),
            pl.BlockSpec((1, D_MODEL), lambda i: (0, 0)),
            pl.BlockSpec((D_MODEL, N_MAIN), lambda i: (0, 0), pipeline_mode=pl.Buffered(1)),
            pl.BlockSpec((D_MODEL, LANES), lambda i: (0, 0), pipeline_mode=pl.Buffered(1)),
        ],
        out_specs=[
            pl.BlockSpec((tm, N_MAIN), lambda i: (i, 0)),
            pl.BlockSpec((tm, LANES), lambda i: (i, 0)),
        ],
        out_shape=[
            jax.ShapeDtypeStruct((m, N_MAIN), out_dtype),
            jax.ShapeDtypeStruct((m, LANES), F32),
        ],
        scratch_shapes=[pltpu.VMEM((tm, D_MODEL), BF16)],
        compiler_params=pltpu.CompilerParams(
            dimension_semantics=("arbitrary",), vmem_limit_bytes=VMEM_LIMIT_BYTES),
        name="inproj",
    )(x2d, norm_w, w_main, w_dt)


def _outproj_body(x_ref, g_ref, oa_ref, os_ref, wa_ref, ws_ref, wo_ref, y_ref):
    pa = jnp.dot(oa_ref[...].astype(BF16), wa_ref[...], preferred_element_type=F32)
    ps = jnp.dot(os_ref[...].astype(BF16), ws_ref[...], preferred_element_type=F32)
    g = g_ref[...].astype(F32)
    merged = _sigmoid(g[:, :D_MODEL]) * pa + _sigmoid(g[:, D_MODEL:]) * ps
    y_ref[...] = x_ref[...] + jnp.dot(merged.astype(BF16), wo_ref[...], preferred_element_type=F32)


def _outproj(x2d, proj, o_a, o_s, w_ap, w_sp, w_o, tm):
    m = x2d.shape[0]
    const = lambda i: (0, 0)
    return pl.pallas_call(
        _outproj_body,
        grid=(m // tm,),
        in_specs=[
            pl.BlockSpec((tm, D_MODEL), lambda i: (i, 0)),
            pl.BlockSpec((tm, 2 * D_MODEL), lambda i: (i, COL_G // (2 * D_MODEL))),
            pl.BlockSpec((tm, ATTN_WIDTH), lambda i: (i, 0)),
            pl.BlockSpec((tm, SSD_WIDTH), lambda i: (i, 0)),
            pl.BlockSpec((ATTN_WIDTH, D_MODEL), const, pipeline_mode=pl.Buffered(1)),
            pl.BlockSpec((SSD_WIDTH, D_MODEL), const, pipeline_mode=pl.Buffered(1)),
            pl.BlockSpec((D_MODEL, D_MODEL), const, pipeline_mode=pl.Buffered(1)),
        ],
        out_specs=pl.BlockSpec((tm, D_MODEL), lambda i: (i, 0)),
        out_shape=jax.ShapeDtypeStruct((m, D_MODEL), F32),
        compiler_params=pltpu.CompilerParams(
            dimension_semantics=("arbitrary",), vmem_limit_bytes=VMEM_LIMIT_BYTES),
        name="outproj",
    )(x2d, proj, o_a, o_s, w_ap, w_sp, w_o)


def _norm_rope(xc, nw, bd2, cos_t, sin_lo, sin_hi):
    hi, lo = _split2(xc * xc)
    ms = jnp.dot(jnp.concatenate([hi, lo], axis=1), bd2, preferred_element_type=F32)
    xn = xc * lax.rsqrt(ms + EPS) * nw
    return xn * cos_t + pltpu.roll(xn, LANES - ROPE_DIM // 2, 1) * sin_lo + pltpu.roll(xn, ROPE_DIM // 2, 1) * sin_hi


def _dup_half(chunk, keep_low, lane_lo):
    rolled = pltpu.roll(chunk, HALF, 1)
    return jnp.where(lane_lo, chunk, rolled) if keep_low else jnp.where(lane_lo, rolled, chunk)


def _attn_body(*refs, seqs, t, decode):
    if decode:
        (sinks_ref, q_ref, kv_ref, za_ref, cos_ref, slo_ref, shi_ref, qnw_ref, knw_ref, bd2_ref,
         ck_ref, cv_ref, oa_ref, kwin_ref, vwin_ref, kdup, vdup, o_scr) = refs
    else:
        (sinks_ref, q_ref, kv_ref, za_ref, cos_ref, slo_ref, shi_ref, qnw_ref, knw_ref, bd2_ref,
         oa_ref, kwin_ref, vwin_ref, kdup, vdup, o_scr) = refs
    blk = pl.program_id(1)
    rows = seqs * t
    tp = max(t, 2 * SUBLANES)
    nkeys = 2 * WINDOW
    bd2 = bd2_ref[...]
    cos_t, sin_lo, sin_hi = cos_ref[...], slo_ref[...], shi_ref[...]
    lane_lo = lax.broadcasted_iota(jnp.int32, (1, LANES), 1) < HALF

    kv = kv_ref[...].astype(F32)
    k_rot = [_norm_rope(kv[:, c * LANES:(c + 1) * LANES], knw_ref[...], bd2, cos_t, sin_lo, sin_hi)
             for c in range(KV_WIDTH // LANES)]
    v_raw = [kv[:, KV_WIDTH + c * LANES:KV_WIDTH + (c + 1) * LANES] for c in range(KV_WIDTH // LANES)]
    q_all = q_ref[...].astype(F32)
    q_rot = [_norm_rope(q_all[:, c * LANES:(c + 1) * LANES], qnw_ref[...], bd2, cos_t, sin_lo, sin_hi)
             * (HEAD_DIM ** -0.5) for c in range(ATTN_WIDTH // LANES)]

    if not decode:
        @pl.when(blk == 0)
        def _():
            kdup[:, :, 0:WINDOW, :] = jnp.zeros((seqs, N_KV_HEADS, WINDOW, LANES), BF16)
            vdup[:, :, 0:WINDOW, :] = jnp.zeros((seqs, N_KV_HEADS, WINDOW, LANES), BF16)

    qi = lax.broadcasted_iota(jnp.int32, (t, nkeys), 0)
    kj = lax.broadcasted_iota(jnp.int32, (t, nkeys), 1)
    valid = (kj > qi) & (kj <= qi + WINDOW)
    if not decode:
        valid = valid & ((kj >= WINDOW) | (blk > 0))

    def pad_rows(a):
        if tp == t:
            return a
        return jnp.concatenate([a, jnp.zeros((tp - t, a.shape[1]), a.dtype)], axis=0)

    for si in range(seqs):
        rs = slice(si * t, (si + 1) * t)
        for g in range(N_KV_HEADS):
            c, low = g // 2, (g % 2 == 0)
            kdup[si, g, WINDOW:WINDOW + tp, :] = pad_rows(_dup_half(k_rot[c][rs], low, lane_lo)).astype(BF16)
            vdup[si, g, WINDOW:WINDOW + tp, :] = pad_rows(_dup_half(v_raw[c][rs], low, lane_lo)).astype(BF16)
            if decode:
                kdup[si, g, 0:WINDOW, :] = _dup_half(ck_ref[si, :, c * LANES:(c + 1) * LANES], low, lane_lo).astype(BF16)
                vdup[si, g, 0:WINDOW, :] = _dup_half(cv_ref[si, :, c * LANES:(c + 1) * LANES], low, lane_lo).astype(BF16)
                kdup[si, g, WINDOW + tp:nkeys, :] = jnp.zeros((nkeys - WINDOW - tp, LANES), BF16)
                vdup[si, g, WINDOW + tp:nkeys, :] = jnp.zeros((nkeys - WINDOW - tp, LANES), BF16)
        for g in range(N_KV_HEADS):
            heads = range(KV_GROUP * g, KV_GROUP * (g + 1))
            q_stack = jnp.concatenate(
                [jnp.where(lane_lo if j % 2 == 0 else jnp.logical_not(lane_lo), q_rot[j // 2][rs], 0.0)
                 for j in heads], axis=0).astype(BF16)
            s = lax.dot_general(q_stack, kdup[si, g], (((1,), (1,)), ((), ())), preferred_element_type=F32)
            p_parts, den_parts = [], []
            for jj, j in enumerate(heads):
                sj = jnp.where(valid, s[jj * t:(jj + 1) * t], NEG_BIG)
                sink = sinks_ref[j]
                mx = jnp.maximum(jnp.max(sj, axis=-1, keepdims=True), sink)
                p = jnp.exp(sj - mx)
                den_parts.append(jnp.sum(p, axis=-1, keepdims=True) + jnp.exp(sink - mx))
                p_parts.append(p)
            p_all = jnp.concatenate(p_parts, axis=0).astype(BF16)
            o = jnp.dot(p_all, vdup[si, g], preferred_element_type=F32) / jnp.concatenate(den_parts, axis=0)
            for half in range(2):
                o_scr[rs, (2 * g + half) * LANES:(2 * g + half + 1) * LANES] = jnp.where(
                    lane_lo, o[(2 * half) * t:(2 * half + 1) * t], o[(2 * half + 1) * t:(2 * half + 2) * t])

    oa_ref[...] = (o_scr[...] * _silu(za_ref[...].astype(F32))).astype(oa_ref.dtype)

    k_new = jnp.concatenate(k_rot, axis=1)
    v_new = jnp.concatenate(v_raw, axis=1)
    if decode:
        for si in range(seqs):
            kwin_ref[si, 0:WINDOW - t, :] = ck_ref[si, t:WINDOW, :]
            vwin_ref[si, 0:WINDOW - t, :] = cv_ref[si, t:WINDOW, :]
            kwin_ref[si, WINDOW - t:WINDOW, :] = k_new[si * t:(si + 1) * t]
            vwin_ref[si, WINDOW - t:WINDOW, :] = v_new[si * t:(si + 1) * t]
    else:
        @pl.when(blk == pl.num_programs(1) - 1)
        def _():
            kwin_ref[0] = k_new
            vwin_ref[0] = v_new

        kdup[:, :, 0:WINDOW, :] = kdup[:, :, WINDOW:nkeys, :]
        vdup[:, :, 0:WINDOW, :] = vdup[:, :, WINDOW:nkeys, :]


def _attention(proj, sinks, tables, qnw2, knw2, bd2, nseq, nblk, seqs, t, decode, caches, out_dtype):
    rows = seqs * t
    m = proj.shape[0]
    row_blk = lambda s, i: s * nblk + i
    tab_map = (lambda s, i: (0, 0)) if decode else (lambda s, i: (i, 0))
    const = lambda s, i: (0, 0)
    in_specs = [
        pl.BlockSpec(memory_space=pltpu.SMEM),
        pl.BlockSpec((rows, ATTN_WIDTH), lambda s, i: (row_blk(s, i), COL_Q // ATTN_WIDTH)),
        pl.BlockSpec((rows, 2 * KV_WIDTH), lambda s, i: (row_blk(s, i), COL_KV // (2 * KV_WIDTH))),
        pl.BlockSpec((rows, ATTN_WIDTH), lambda s, i: (row_blk(s, i), COL_ZA // ATTN_WIDTH)),
        pl.BlockSpec((rows, LANES), tab_map),
        pl.BlockSpec((rows, LANES), tab_map),
        pl.BlockSpec((rows, LANES), tab_map),
        pl.BlockSpec((1, LANES), const),
        pl.BlockSpec((1, LANES), const),
        pl.BlockSpec((2 * LANES, LANES), const),
    ]
    args = [sinks, proj, proj, proj, *tables, qnw2, knw2, bd2]
    if decode:
        in_specs += [pl.BlockSpec((seqs, WINDOW, KV_WIDTH), lambda s, i: (s, 0, 0))] * 2
        args += list(caches)
    nwin = nseq * seqs
    return pl.pallas_call(
        functools.partial(_attn_body, seqs=seqs, t=t, decode=decode),
        grid=(nseq, nblk),
        in_specs=in_specs,
        out_specs=[
            pl.BlockSpec((rows, ATTN_WIDTH), lambda s, i: (row_blk(s, i), 0)),
            pl.BlockSpec((seqs, WINDOW, KV_WIDTH), lambda s, i: (s, 0, 0)),
            pl.BlockSpec((seqs, WINDOW, KV_WIDTH), lambda s, i: (s, 0, 0)),
        ],
        out_shape=[
            jax.ShapeDtypeStruct((m, ATTN_WIDTH), out_dtype),
            jax.ShapeDtypeStruct((nwin, WINDOW, KV_WIDTH), F32),
            jax.ShapeDtypeStruct((nwin, WINDOW, KV_WIDTH), F32),
        ],
        scratch_shapes=[
            pltpu.VMEM((seqs, N_KV_HEADS, 2 * WINDOW, LANES), BF16),
            pltpu.VMEM((seqs, N_KV_HEADS, 2 * WINDOW, LANES), BF16),
            pltpu.VMEM((rows, ATTN_WIDTH), F32),
        ],
        compiler_params=pltpu.CompilerParams(
            dimension_semantics=("arbitrary", "arbitrary"), vmem_limit_bytes=VMEM_LIMIT_BYTES),
        name="attn_decode" if decode else "attn_prompt",
    )(*args)


def _ssd_body(*refs, q, decode):
    if decode:
        (xbc_ref, zs_ref, dt_ref, cw_ref, cb_ref, dtb_ref, alog_ref, dsk_ref, nw_ref, e2_ref, tri_ref,
         cst_ref, h0_ref, os_ref, cout_ref, hout_ref, xp, xs_scr, bc_scr, acs_scr, acst_scr, dtt_scr, ht, y_scr) = refs
    else:
        (xbc_ref, zs_ref, dt_ref, cw_ref, cb_ref, dtb_ref, alog_ref, dsk_ref, nw_ref, e2_ref, tri_ref,
         os_ref, cout_ref, hout_ref, xp, xs_scr, bc_scr, acs_scr, acst_scr, dtt_scr, ht, y_scr) = refs
    chunk = pl.program_id(1)
    lc = SSD_CHUNK
    tail = CONV_WIDTH - 1
    base = SUBLANES

    if decode:
        xp[base - tail:base, :] = cst_ref[0]
        for g in range(SSD_GROUPS):
            ht[g] = h0_ref[0, g * GROUP_WIDTH:(g + 1) * GROUP_WIDTH, :].T
    else:
        @pl.when(chunk == 0)
        def _():
            xp[0:base, :] = jnp.zeros((base, CONV_DIM), F32)
            ht[...] = jnp.zeros(ht.shape, F32)
    xp[base:base + q, :] = xbc_ref[...].astype(F32)

    slab = 4 * LANES
    for cs in range(CONV_DIM // slab):
        cl = slice(cs * slab, (cs + 1) * slab)
        acc = cb_ref[:, cl] + cw_ref[0:1, cl] * xp[base - tail:base - tail + q, cl]
        for j in range(1, CONV_WIDTH):
            acc = acc + cw_ref[j:j + 1, cl] * xp[base - tail + j:base - tail + j + q, cl]
        act = _silu(acc)
        if cs < SSD_WIDTH // slab:
            xs_scr[0:q, cl] = act
        else:
            bc_scr[0:q, cs * slab - SSD_WIDTH:(cs + 1) * slab - SSD_WIDTH] = act
    if q < lc:
        xs_scr[q:lc, :] = jnp.zeros((lc - q, SSD_WIDTH), F32)
        bc_scr[q:lc, :] = jnp.zeros((lc - q, 2 * SSD_GROUPS * D_STATE), F32)
    new_tail = xp[base + q - tail:base + q, :]
    if decode:
        cout_ref[0] = new_tail
    else:
        @pl.when(chunk == pl.num_programs(1) - 1)
        def _():
            cout_ref[0] = new_tail
        xp[base - tail:base, :] = new_tail

    dt_in = dt_ref[...]
    if q < lc:
        dt_in = jnp.concatenate([dt_in, jnp.zeros((lc - q, LANES), F32)], axis=0)
    xdt = dt_in + dtb_ref[...]
    dt = jnp.maximum(xdt, 0.0) + jnp.log1p(jnp.exp(-jnp.abs(xdt)))
    if q < lc:
        dt = jnp.where(lax.broadcasted_iota(jnp.int32, (lc, LANES), 0) < q, dt, 0.0)
    a = dt * (-jnp.exp(alog_ref[...]))
    a_hi = a.astype(BF16)
    a_r1 = a - a_hi.astype(F32)
    a_mid = a_r1.astype(BF16)
    a_lo = (a_r1 - a_mid.astype(F32)).astype(BF16)
    acs3 = jnp.dot(tri_ref[...], jnp.concatenate([a_hi, a_mid, a_lo], axis=1), preferred_element_type=F32)
    acs = acs3[:, 0:LANES] + acs3[:, LANES:2 * LANES] + acs3[:, 2 * LANES:3 * LANES]
    acs_scr[...] = acs
    acst_scr[...] = acs.T
    dtt_scr[...] = dt.T

    def expand(v):
        hi, lo = _split2(v)
        return jnp.dot(jnp.concatenate([hi, lo], axis=1), e2_ref[...], preferred_element_type=F32)

    e_off = expand(jnp.exp(acs))
    w_state = expand(jnp.exp(acs[lc - 1:lc, :] - acs) * dt)
    chunk_decay = e_off[lc - 1:lc, :]

    causal = (lax.broadcasted_iota(jnp.int32, (q, lc), 0) >= lax.broadcasted_iota(jnp.int32, (q, lc), 1))
    lane_lo = lax.broadcasted_iota(jnp.int32, (1, LANES), 1) < HALF
    nbc = SSD_GROUPS * D_STATE
    for g in range(SSD_GROUPS):
        b_g = bc_scr[:, g * D_STATE:(g + 1) * D_STATE]
        c_g = bc_scr[0:q, nbc + g * D_STATE:nbc + (g + 1) * D_STATE].astype(BF16)
        gl = slice(g * GROUP_WIDTH, (g + 1) * GROUP_WIDTH)
        cb = lax.dot_general(c_g, b_g.astype(BF16), (((1,), (1,)), ((), ())), preferred_element_type=F32)
        y_off = jnp.dot(c_g, ht[g].astype(BF16), preferred_element_type=F32) * e_off[0:q, gl]
        y_scr[:, gl] = y_off + dsk_ref[:, gl] * xs_scr[0:q, gl]
        for pr in range(SSD_HPG // 2):
            pair = g * (SSD_HPG // 2) + pr
            w_parts = []
            for h in (2 * pair, 2 * pair + 1):
                seg = acs_scr[0:q, h:h + 1] - acst_scr[h:h + 1, :]
                decay = jnp.exp(jnp.where(causal, seg, -jnp.inf))
                w_parts.append(cb * decay * dtt_scr[h:h + 1, :])
            w_pair = jnp.concatenate(w_parts, axis=1).astype(BF16)
            x_pair = xs_scr[:, pair * LANES:(pair + 1) * LANES]
            x_bd = jnp.concatenate([jnp.where(lane_lo, x_pair, 0.0), jnp.where(lane_lo, 0.0, x_pair)],
                                   axis=0).astype(BF16)
            pl_ = slice(pair * LANES, (pair + 1) * LANES)
            y_scr[:, pl_] = y_scr[:, pl_] + jnp.dot(w_pair, x_bd, preferred_element_type=F32)
        xw = (xs_scr[:, gl] * w_state[:, gl]).astype(BF16)
        ht[g] = ht[g] * chunk_decay[:, gl] + jnp.dot(b_g.T.astype(BF16), xw, preferred_element_type=F32)

    y = y_scr[...] * _silu(zs_ref[...].astype(F32))
    for g in range(SSD_GROUPS):
        gl = slice(g * GROUP_WIDTH, (g + 1) * GROUP_WIDTH)
        yg = y[:, gl]
        ms = jnp.mean(yg * yg, axis=-1, keepdims=True)
        os_ref[:, gl] = (yg * lax.rsqrt(ms + EPS) * nw_ref[:, gl]).astype(os_ref.dtype)

    def write_state():
        for g in range(SSD_GROUPS):
            hout_ref[0, g * GROUP_WIDTH:(g + 1) * GROUP_WIDTH, :] = ht[g].T

    if decode:
        write_state()
    else:
        pl.when(chunk == pl.num_programs(1) - 1)(write_state)


def _ssd(proj, dt_raw, consts, nseq, nchunk, q, decode, states, out_dtype):
    m = proj.shape[0]
    row_blk = lambda s, c: s * nchunk + c
    const = lambda s, c: (0, 0)
    in_specs = [
        pl.BlockSpec((q, CONV_DIM), lambda s, c: (row_blk(s, c), COL_XBC // CONV_DIM)),
        pl.BlockSpec((q, SSD_WIDTH), lambda s, c: (row_blk(s, c), COL_ZS // SSD_WIDTH)),
        pl.BlockSpec((q, LANES), lambda s, c: (row_blk(s, c), 0)),
        pl.BlockSpec((CONV_WIDTH, CONV_DIM), const),
        pl.BlockSpec((1, CONV_DIM), const),
        pl.BlockSpec((1, LANES), const),
        pl.BlockSpec((1, LANES), const),
        pl.BlockSpec((1, SSD_WIDTH), const),
        pl.BlockSpec((1, SSD_WIDTH), const),
        pl.BlockSpec((2 * LANES, SSD_WIDTH), const),
        pl.BlockSpec((SSD_CHUNK, SSD_CHUNK), const),
    ]
    args = [proj, proj, dt_raw, *consts]
    if decode:
        in_specs += [
            pl.BlockSpec((1, CONV_WIDTH - 1, CONV_DIM), lambda s, c: (s, 0, 0)),
            pl.BlockSpec((1, SSD_WIDTH, D_STATE), lambda s, c: (s, 0, 0)),
        ]
        args += list(states)
    return pl.pallas_call(
        functools.partial(_ssd_body, q=q, decode=decode),
        grid=(nseq, nchunk),
        in_specs=in_specs,
        out_specs=[
            pl.BlockSpec((q, SSD_WIDTH), lambda s, c: (row_blk(s, c), 0)),
            pl.BlockSpec((1, CONV_WIDTH - 1, CONV_DIM), lambda s, c: (s, 0, 0)),
            pl.BlockSpec((1, SSD_WIDTH, D_STATE), lambda s, c: (s, 0, 0)),
        ],
        out_shape=[
            jax.ShapeDtypeStruct((m, SSD_WIDTH), out_dtype),
            jax.ShapeDtypeStruct((nseq, CONV_WIDTH - 1, CONV_DIM), F32),
            jax.ShapeDtypeStruct((nseq, SSD_WIDTH, D_STATE), F32),
        ],
        scratch_shapes=[
            pltpu.VMEM((SUBLANES + SSD_CHUNK, CONV_DIM), F32),
            pltpu.VMEM((SSD_CHUNK, SSD_WIDTH), F32),
            pltpu.VMEM((SSD_CHUNK, 2 * SSD_GROUPS * D_STATE), F32),
            pltpu.VMEM((SSD_CHUNK, LANES), F32),
            pltpu.VMEM((LANES, SSD_CHUNK), F32),
            pltpu.VMEM((LANES, SSD_CHUNK), F32),
            pltpu.VMEM((SSD_GROUPS, D_STATE, GROUP_WIDTH), F32),
            pltpu.VMEM((q, SSD_WIDTH), F32),
        ],
        compiler_params=pltpu.CompilerParams(
            dimension_semantics=("arbitrary", "arbitrary"), vmem_limit_bytes=VMEM_LIMIT_BYTES),
        name="ssd_decode" if decode else "ssd_prompt",
    )(*args)


def _rope_tables(pos):
    half = ROPE_DIM // 2
    inv_freq = jnp.power(ROPE_THETA, -jnp.arange(half, dtype=F32) * (2.0 / ROPE_DIM))
    ang = pos.astype(F32)[:, None] * inv_freq[None, :]
    cos, sin = jnp.cos(ang), jnp.sin(ang)
    n = pos.shape[0]
    rest = HEAD_DIM - ROPE_DIM
    z_half, z_rest = jnp.zeros((n, half), F32), jnp.zeros((n, rest), F32)
    cos_t = jnp.concatenate([cos, cos, jnp.ones((n, rest), F32)], axis=1)
    sin_lo = jnp.concatenate([-sin, z_half, z_rest], axis=1)
    sin_hi = jnp.concatenate([z_half, sin, z_rest], axis=1)
    return tuple(jnp.tile(tb, (1, LANES // HEAD_DIM)) for tb in (cos_t, sin_lo, sin_hi))


def _pad_lanes(v):
    return jnp.pad(v.reshape(1, -1).astype(F32), ((0, 0), (0, LANES - v.shape[-1])))


def kernel(x_prompt, x_sample, cache_k, cache_v, state_conv, state_ssm, norm_w, w_in, q_norm_w, k_norm_w, sinks,
           conv_w, conv_b, dt_bias, A_log, D_skip, ssd_norm_w, w_attn_proj, w_ssd_proj, w_out):
    depth = norm_w.shape[0]
    assert depth == 1, "single-layer trunk"
    bp, seq = x_prompt.shape[:2]
    bs, tdec = x_sample.shape[:2]
    assert seq % WINDOW == 0 and tdec == SUBLANES and cache_k.shape[2] == WINDOW

    w = w_in[0]
    w_main = jnp.concatenate([
        w[:, OFF_ZA:OFF_ZS], w[:, OFF_DT:], w[:, :OFF_Q], w[:, OFF_V:OFF_ZA], w[:, OFF_ZS:OFF_XBC], w[:, OFF_Q:OFF_V],
    ], axis=1).astype(BF16)
    w_dt = jnp.pad(w[:, OFF_XBC:OFF_DT], ((0, 0), (0, LANES - SSD_HEADS))).astype(BF16)
    nw = norm_w[0].reshape(1, D_MODEL)
    w_ap, w_sp, w_o = w_attn_proj[0].astype(BF16), w_ssd_proj[0].astype(BF16), w_out[0].astype(BF16)

    lane = jnp.arange(LANES)
    bd = (lane[:, None] // HEAD_DIM == lane[None, :] // HEAD_DIM).astype(F32) / HEAD_DIM
    bd2 = jnp.concatenate([bd, bd], axis=0).astype(BF16)
    head_of_lane = jnp.arange(SSD_WIDTH) // SSD_HEAD_DIM
    e1 = (lane[:, None] == head_of_lane[None, :]).astype(BF16)
    e2 = jnp.concatenate([e1, e1], axis=0)
    tri = (lane[:, None] >= lane[None, :]).astype(BF16)
    qnw2 = jnp.tile(q_norm_w[0].reshape(1, HEAD_DIM), (1, LANES // HEAD_DIM))
    knw2 = jnp.tile(k_norm_w[0].reshape(1, HEAD_DIM), (1, LANES // HEAD_DIM))
    ssd_consts = (conv_w[0], conv_b[0].reshape(1, CONV_DIM), _pad_lanes(dt_bias[0]), _pad_lanes(A_log[0]),
                  jnp.repeat(D_skip[0], SSD_HEAD_DIM).reshape(1, SSD_WIDTH), ssd_norm_w[0].reshape(1, SSD_WIDTH), e2, tri)
    sink_vec = sinks[0].astype(F32)

    xp2 = x_prompt.reshape(bp * seq, D_MODEL)
    nblk = seq // WINDOW
    proj_p, dt_p = _inproj(xp2, nw, w_main, w_dt, 512, BF16)
    tables_p = _rope_tables(jnp.arange(seq, dtype=jnp.int32))
    oa_p, kwin_p, vwin_p = _attention(proj_p, sink_vec, tables_p, qnw2, knw2, bd2, bp, nblk, 1, WINDOW, False, None, BF16)
    os_p, conv_p, ssm_p = _ssd(proj_p, dt_p, ssd_consts, bp, seq // SSD_CHUNK, SSD_CHUNK, False, None, BF16)
    y_p = _outproj(xp2, proj_p, oa_p, os_p, w_ap, w_sp, w_o, 512).reshape(bp, seq, D_MODEL)

    xs2 = x_sample.reshape(bs * tdec, D_MODEL)
    seqs_per_step = 4
    proj_s, dt_s = _inproj(xs2, nw, w_main, w_dt, 256, F32)
    pos_s = PAST_LEN + jnp.arange(tdec, dtype=jnp.int32)
    tables_s = tuple(jnp.tile(tb, (seqs_per_step, 1)) for tb in _rope_tables(pos_s))
    ck = cache_k[0].reshape(bs, WINDOW, KV_WIDTH)
    cv = cache_v[0].reshape(bs, WINDOW, KV_WIDTH)
    oa_s, kwin_s, vwin_s = _attention(proj_s, sink_vec, tables_s, qnw2, knw2, bd2, bs // seqs_per_step, 1,
                                      seqs_per_step, tdec, True, (ck, cv), F32)
    h0 = state_ssm[0].reshape(bs, SSD_WIDTH, D_STATE)
    os_s, conv_s, ssm_s = _ssd(proj_s, dt_s, ssd_consts, bs, 1, tdec, True, (state_conv[0], h0), F32)
    y_s = _outproj(xs2, proj_s, oa_s, os_s, w_ap, w_sp, w_o, 256).reshape(bs, tdec, D_MODEL)

    win = lambda a, n: a.reshape(1, n, WINDOW, N_KV_HEADS, HEAD_DIM)
    ssm = lambda a, n: a.reshape(1, n, SSD_HEADS, SSD_HEAD_DIM, D_STATE)
    return (y_p, y_s, win(kwin_p, bp), win(vwin_p, bp), conv_p[None], ssm(ssm_p, bp),
            win(kwin_s, bs), win(vwin_s, bs), conv_s[None], ssm(ssm_s, bs))
```

```python
import functools

import jax
import jax.numpy as jnp
from jax import lax
from jax.experimental import pallas as pl
from jax.experimental.pallas import tpu as pltpu

F32 = jnp.float32
BF16 = jnp.bfloat16

D_MODEL = 1024
N_HEADS = 16
N_KV_HEADS = 4
HEAD_DIM = 64
KV_GROUP = N_HEADS // N_KV_HEADS
ATTN_WIDTH = N_HEADS * HEAD_DIM
KV_WIDTH = N_KV_HEADS * HEAD_DIM
WINDOW = 128
ROPE_DIM = HEAD_DIM // 4
ROPE_THETA = 500000.0
PAST_LEN = 16384
SSD_WIDTH = 2 * D_MODEL
SSD_HEAD_DIM = 64
SSD_HEADS = SSD_WIDTH // SSD_HEAD_DIM
SSD_GROUPS = 4
SSD_HPG = SSD_HEADS // SSD_GROUPS
D_STATE = 128
CONV_WIDTH = 4
CONV_DIM = SSD_WIDTH + 2 * SSD_GROUPS * D_STATE
SSD_CHUNK = 128
EPS = 1e-6

OFF_Q = ATTN_WIDTH
OFF_K = OFF_Q + KV_WIDTH
OFF_V = OFF_K + KV_WIDTH
OFF_ZA = OFF_V + ATTN_WIDTH
OFF_ZS = OFF_ZA + SSD_WIDTH
OFF_XBC = OFF_ZS + CONV_DIM
OFF_DT = OFF_XBC + SSD_HEADS

LANES = 128
SUBLANES = 8
VMEM_LIMIT_BYTES = 56 * 1024 * 1024

COL_ZS = 0
COL_G = COL_ZS + SSD_WIDTH
COL_Q = COL_G + 2 * D_MODEL
COL_ZA = COL_Q + ATTN_WIDTH
COL_XBC = COL_ZA + ATTN_WIDTH
COL_KV = COL_XBC + CONV_DIM
N_MAIN = COL_KV + 2 * KV_WIDTH
PROJ_TILE_N = 512
GROUP_WIDTH = SSD_WIDTH // SSD_GROUPS
HALF = LANES // 2
NEG_BIG = -1e30
LOG2E = 1.4426950408889634


def _sigmoid(x):
    return 0.5 * jnp.tanh(0.5 * x) + 0.5


def _silu(x):
    hx = 0.5 * x
    return hx * jnp.tanh(hx) + hx


def _split2(v):
    hi = v.astype(BF16)
    lo = (v - hi.astype(F32)).astype(BF16)
    return hi, lo


def _inproj_body(x_ref, nw_ref, w_ref, wdt_ref, out_ref, dt_ref, h_scr):
    x = x_ref[...]
    ms = jnp.mean(x * x, axis=-1, keepdims=True)
    h_scr[...] = (x * lax.rsqrt(ms + EPS) * nw_ref[...]).astype(BF16)
    for j in range(N_MAIN // PROJ_TILE_N):
        sl = slice(j * PROJ_TILE_N, (j + 1) * PROJ_TILE_N)
        out_ref[:, sl] = jnp.dot(h_scr[...], w_ref[:, sl], preferred_element_type=F32).astype(out_ref.dtype)
    dt_ref[...] = jnp.dot(h_scr[...], wdt_ref[...], preferred_element_type=F32)


def _inproj(x2d, norm_w, w_main, w_dt, tm, out_dtype):
    m = x2d.shape[0]
    return pl.pallas_call(
        _inproj_body,
        grid=(m // tm,),
        in_specs=[
            pl.BlockSpec((tm, D_MODEL), lambda i: (i, 0)),
            pl.BlockSpec((1, D_MODEL), lambda i: (0, 0)),
            pl.BlockSpec((D_MODEL, N_MAIN), lambda i: (0, 0), pipeline_mode=pl.Buffered(1)),
            pl.BlockSpec((D_MODEL, LANES), lambda i: (0, 0), pipeline_mode=pl.Buffered(1)),
        ],
        out_specs=[
            pl.BlockSpec((tm, N_MAIN), lambda i: (i, 0)),
            pl.BlockSpec((tm, LANES), lambda i: (i, 0)),
        ],
        out_shape=[
            jax.ShapeDtypeStruct((m, N_MAIN), out_dtype),
            jax.ShapeDtypeStruct((m, LANES), F32),
        ],
        scratch_shapes=[pltpu.VMEM((tm, D_MODEL), BF16)],
        compiler_params=pltpu.CompilerParams(
            dimension_semantics=("arbitrary",), vmem_limit_bytes=VMEM_LIMIT_BYTES),
        name="inproj",
    )(x2d, norm_w, w_main, w_dt)


def _outproj_body(x_ref, g_ref, oa_ref, os_ref, wa_ref, ws_ref, wo_ref, y_ref):
    pa = jnp.dot(oa_ref[...].astype(BF16), wa_ref[...], preferred_element_type=F32)
    ps = jnp.dot(os_ref[...].astype(BF16), ws_ref[...], preferred_element_type=F32)
    g = g_ref[...].astype(F32)
    merged = _sigmoid(g[:, :D_MODEL]) * pa + _sigmoid(g[:, D_MODEL:]) * ps
    y_ref[...] = x_ref[...] + jnp.dot(merged.astype(BF16), wo_ref[...], preferred_element_type=F32)


def _outproj(x2d, proj, o_a, o_s, w_ap, w_sp, w_o, tm):
    m = x2d.shape[0]
    const = lambda i: (0, 0)
    return pl.pallas_call(
        _outproj_body,
        grid=(m // tm,),
        in_specs=[
            pl.BlockSpec((tm, D_MODEL), lambda i: (i, 0)),
            pl.BlockSpec((tm, 2 * D_MODEL), lambda i: (i, COL_G // (2 * D_MODEL))),
            pl.BlockSpec((tm, ATTN_WIDTH), lambda i: (i, 0)),
            pl.BlockSpec((tm, SSD_WIDTH), lambda i: (i, 0)),
            pl.BlockSpec((ATTN_WIDTH, D_MODEL), const, pipeline_mode=pl.Buffered(1)),
            pl.BlockSpec((SSD_WIDTH, D_MODEL), const, pipeline_mode=pl.Buffered(1)),
            pl.BlockSpec((D_MODEL, D_MODEL), const, pipeline_mode=pl.Buffered(1)),
        ],
        out_specs=pl.BlockSpec((tm, D_MODEL), lambda i: (i, 0)),
        out_shape=jax.ShapeDtypeStruct((m, D_MODEL), F32),
        compiler_params=pltpu.CompilerParams(
            dimension_semantics=("arbitrary",), vmem_limit_bytes=VMEM_LIMIT_BYTES),
        name="outproj",
    )(x2d, proj, o_a, o_s, w_ap, w_sp, w_o)


def _norm_rope(xc, nw, bd2, cos_t, sin_lo, sin_hi):
    hi, lo = _split2(xc * xc)
    ms = jnp.dot(jnp.concatenate([hi, lo], axis=1), bd2, preferred_element_type=F32)
    xn = xc * lax.rsqrt(ms + EPS) * nw
    return xn * cos_t + pltpu.roll(xn, LANES - ROPE_DIM // 2, 1) * sin_lo + pltpu.roll(xn, ROPE_DIM // 2, 1) * sin_hi


def _dup_half(chunk, keep_low, lane_lo):
    rolled = pltpu.roll(chunk, HALF, 1)
    return jnp.where(lane_lo, chunk, rolled) if keep_low else jnp.where(lane_lo, rolled, chunk)


def _attn_body(*refs, seqs, t, decode):
    if decode:
        (sinks_ref, q_ref, kv_ref, za_ref, cos_ref, slo_ref, shi_ref, qnw_ref, knw_ref, bd2_ref,
         ck_ref, cv_ref, oa_ref, kwin_ref, vwin_ref, kdup, vdup, o_scr) = refs
    else:
        (sinks_ref, q_ref, kv_ref, za_ref, cos_ref, slo_ref, shi_ref, qnw_ref, knw_ref, bd2_ref,
         oa_ref, kwin_ref, vwin_ref, kdup, vdup, o_scr) = refs
    blk = pl.program_id(1)
    rows = seqs * t
    tp = max(t, 2 * SUBLANES)
    nkeys = 2 * WINDOW
    bd2 = bd2_ref[...]
    cos_t, sin_lo, sin_hi = cos_ref[...], slo_ref[...], shi_ref[...]
    lane_lo = lax.broadcasted_iota(jnp.int32, (1, LANES), 1) < HALF

    kv = kv_ref[...].astype(F32)
    k_rot = [_norm_rope(kv[:, c * LANES:(c + 1) * LANES], knw_ref[...], bd2, cos_t, sin_lo, sin_hi)
             for c in range(KV_WIDTH // LANES)]
    v_raw = [kv[:, KV_WIDTH + c * LANES:KV_WIDTH + (c + 1) * LANES] for c in range(KV_WIDTH // LANES)]
    q_all = q_ref[...].astype(F32)
    q_rot = [_norm_rope(q_all[:, c * LANES:(c + 1) * LANES], qnw_ref[...], bd2, cos_t, sin_lo, sin_hi)
             * (HEAD_DIM ** -0.5) for c in range(ATTN_WIDTH // LANES)]

    ones_half = jnp.ones((seqs, N_KV_HEADS, nkeys, LANES), BF16)
    if decode:
        vdup[:, :, :, LANES:2 * LANES] = ones_half
    else:
        @pl.when(blk == 0)
        def _():
            kdup[:, :, 0:WINDOW, :] = jnp.zeros((seqs, N_KV_HEADS, WINDOW, LANES), BF16)
            vdup[:, :, 0:WINDOW, 0:LANES] = jnp.zeros((seqs, N_KV_HEADS, WINDOW, LANES), BF16)
            vdup[:, :, :, LANES:2 * LANES] = ones_half

    own = (lax.broadcasted_iota(jnp.int32, (t, WINDOW), 1) <= lax.broadcasted_iota(jnp.int32, (t, WINDOW), 0))
    prev_bias = None if decode else jnp.where(blk > 0, 0.0, NEG_BIG)

    def pad_rows(a):
        if tp == t:
            return a
        return jnp.concatenate([a, jnp.zeros((tp - t, a.shape[1]), a.dtype)], axis=0)

    for si in range(seqs):
        rs = slice(si * t, (si + 1) * t)
        for g in range(N_KV_HEADS):
            c, low = g // 2, (g % 2 == 0)
            kdup[si, g, WINDOW:WINDOW + tp, :] = pad_rows(_dup_half(k_rot[c][rs], low, lane_lo)).astype(BF16)
            vdup[si, g, WINDOW:WINDOW + tp, 0:LANES] = pad_rows(_dup_half(v_raw[c][rs], low, lane_lo)).astype(BF16)
            if decode:
                kdup[si, g, 0:WINDOW, :] = _dup_half(ck_ref[si, :, c * LANES:(c + 1) * LANES], low, lane_lo).astype(BF16)
                vdup[si, g, 0:WINDOW, 0:LANES] = _dup_half(cv_ref[si, :, c * LANES:(c + 1) * LANES], low, lane_lo).astype(BF16)
                kdup[si, g, WINDOW + tp:nkeys, :] = jnp.zeros((nkeys - WINDOW - tp, LANES), BF16)
                vdup[si, g, WINDOW + tp:nkeys, 0:LANES] = jnp.zeros((nkeys - WINDOW - tp, LANES), BF16)
        scores = []
        for g in range(N_KV_HEADS):
            q_stack = jnp.concatenate(
                [jnp.where(lane_lo if j % 2 == 0 else jnp.logical_not(lane_lo), q_rot[j // 2][rs], 0.0)
                 for j in range(KV_GROUP * g, KV_GROUP * (g + 1))], axis=0).astype(BF16)
            scores.append(lax.dot_general(q_stack, kdup[si, g], (((1,), (1,)), ((), ())),
                                          preferred_element_type=F32))
        probs, sink_terms = [], []
        for j in range(N_HEADS):
            s = scores[j // KV_GROUP]
            r0 = (j % KV_GROUP) * t
            s_prev = s[r0:r0 + t, 0:WINDOW]
            if prev_bias is not None:
                s_prev = s_prev + prev_bias
            sj = jnp.where(own, s[r0:r0 + t, WINDOW:nkeys], s_prev)
            sink = sinks_ref[j]
            mx = jnp.maximum(jnp.max(sj, axis=-1, keepdims=True), sink)
            p = jnp.exp(sj - mx)
            sink_terms.append(jnp.exp(sink - mx))
            probs.append(jnp.concatenate([jnp.where(own, 0.0, p), jnp.where(own, p, 0.0)], axis=1))
        for g in range(N_KV_HEADS):
            p_all = jnp.concatenate(probs[KV_GROUP * g:KV_GROUP * (g + 1)], axis=0).astype(BF16)
            o_aug = jnp.dot(p_all, vdup[si, g], preferred_element_type=F32)
            o = [o_aug[jj * t:(jj + 1) * t, 0:LANES]
                 / (o_aug[jj * t:(jj + 1) * t, LANES:2 * LANES] + sink_terms[KV_GROUP * g + jj])
                 for jj in range(KV_GROUP)]
            for half in range(2):
                o_scr[rs, (2 * g + half) * LANES:(2 * g + half + 1) * LANES] = jnp.where(
                    lane_lo, o[2 * half], o[2 * half + 1])

    oa_ref[...] = (o_scr[...] * _silu(za_ref[...].astype(F32))).astype(oa_ref.dtype)

    k_new = jnp.concatenate(k_rot, axis=1)
    v_new = jnp.concatenate(v_raw, axis=1)
    if decode:
        for si in range(seqs):
            kwin_ref[si, 0:WINDOW - t, :] = ck_ref[si, t:WINDOW, :]
            vwin_ref[si, 0:WINDOW - t, :] = cv_ref[si, t:WINDOW, :]
            kwin_ref[si, WINDOW - t:WINDOW, :] = k_new[si * t:(si + 1) * t]
            vwin_ref[si, WINDOW - t:WINDOW, :] = v_new[si * t:(si + 1) * t]
    else:
        @pl.when(blk == pl.num_programs(1) - 1)
        def _():
            kwin_ref[0] = k_new
            vwin_ref[0] = v_new

        kdup[:, :, 0:WINDOW, :] = kdup[:, :, WINDOW:nkeys, :]
        vdup[:, :, 0:WINDOW, :] = vdup[:, :, WINDOW:nkeys, :]


def _attention(proj, sinks, tables, qnw2, knw2, bd2, nseq, nblk, seqs, t, decode, caches, out_dtype):
    rows = seqs * t
    m = proj.shape[0]
    row_blk = lambda s, i: s * nblk + i
    tab_map = (lambda s, i: (0, 0)) if decode else (lambda s, i: (i, 0))
    const = lambda s, i: (0, 0)
    in_specs = [
        pl.BlockSpec(memory_space=pltpu.SMEM),
        pl.BlockSpec((rows, ATTN_WIDTH), lambda s, i: (row_blk(s, i), COL_Q // ATTN_WIDTH)),
        pl.BlockSpec((rows, 2 * KV_WIDTH), lambda s, i: (row_blk(s, i), COL_KV // (2 * KV_WIDTH))),
        pl.BlockSpec((rows, ATTN_WIDTH), lambda s, i: (row_blk(s, i), COL_ZA // ATTN_WIDTH)),
        pl.BlockSpec((rows, LANES), tab_map),
        pl.BlockSpec((rows, LANES), tab_map),
        pl.BlockSpec((rows, LANES), tab_map),
        pl.BlockSpec((1, LANES), const),
        pl.BlockSpec((1, LANES), const),
        pl.BlockSpec((2 * LANES, LANES), const),
    ]
    args = [sinks, proj, proj, proj, *tables, qnw2, knw2, bd2]
    if decode:
        in_specs += [pl.BlockSpec((seqs, WINDOW, KV_WIDTH), lambda s, i: (s, 0, 0))] * 2
        args += list(caches)
    nwin = nseq * seqs
    return pl.pallas_call(
        functools.partial(_attn_body, seqs=seqs, t=t, decode=decode),
        grid=(nseq, nblk),
        in_specs=in_specs,
        out_specs=[
            pl.BlockSpec((rows, ATTN_WIDTH), lambda s, i: (row_blk(s, i), 0)),
            pl.BlockSpec((seqs, WINDOW, KV_WIDTH), lambda s, i: (s, 0, 0)),
            pl.BlockSpec((seqs, WINDOW, KV_WIDTH), lambda s, i: (s, 0, 0)),
        ],
        out_shape=[
            jax.ShapeDtypeStruct((m, ATTN_WIDTH), out_dtype),
            jax.ShapeDtypeStruct((nwin, WINDOW, KV_WIDTH), F32),
            jax.ShapeDtypeStruct((nwin, WINDOW, KV_WIDTH), F32),
        ],
        scratch_shapes=[
            pltpu.VMEM((seqs, N_KV_HEADS, 2 * WINDOW, LANES), BF16),
            pltpu.VMEM((seqs, N_KV_HEADS, 2 * WINDOW, 2 * LANES), BF16),
            pltpu.VMEM((rows, ATTN_WIDTH), F32),
        ],
        compiler_params=pltpu.CompilerParams(
            dimension_semantics=("arbitrary", "arbitrary"), vmem_limit_bytes=VMEM_LIMIT_BYTES),
        name="attn_decode" if decode else "attn_prompt",
    )(*args)


def _ssd_body(*refs, q, decode):
    if decode:
        (xbc_ref, zs_ref, dt_ref, cw_ref, cb_ref, dtb_ref, alog_ref, dsk_ref, nw_ref, e2_ref, tri_ref, sh_ref,
         cst_ref, h0_ref, os_ref, cout_ref, hout_ref, xp, xs_scr, bc_scr, acs_scr, acst_scr, dtt_scr, ht, y_scr) = refs
    else:
        (xbc_ref, zs_ref, dt_ref, cw_ref, cb_ref, dtb_ref, alog_ref, dsk_ref, nw_ref, e2_ref, tri_ref, sh_ref,
         os_ref, cout_ref, hout_ref, xp, xs_scr, bc_scr, acs_scr, acst_scr, dtt_scr, ht, y_scr) = refs
    chunk = pl.program_id(1)
    lc = SSD_CHUNK
    tail = CONV_WIDTH - 1
    base = SUBLANES

    if decode:
        xp[base - tail:base, :] = cst_ref[0]
        xp[base:base + q, :] = xbc_ref[...].astype(F32)
        for g in range(SSD_GROUPS):
            ht[g] = h0_ref[0, g * GROUP_WIDTH:(g + 1) * GROUP_WIDTH, :].T
    else:
        @pl.when(chunk == 0)
        def _():
            xp[...] = jnp.zeros((base, CONV_DIM), F32)
            ht[...] = jnp.zeros(ht.shape, F32)

    slab = 4 * LANES
    row8 = lax.broadcasted_iota(jnp.int32, (SUBLANES, 1), 0)
    for cs in range(CONV_DIM // slab):
        cl = slice(cs * slab, (cs + 1) * slab)
        if decode:
            acc = cb_ref[:, cl] + cw_ref[0:1, cl] * xp[base - tail:base - tail + q, cl]
            for j in range(1, CONV_WIDTH):
                acc = acc + cw_ref[j:j + 1, cl] * xp[base - tail + j:base - tail + j + q, cl]
        else:
            x_cur = xbc_ref[:, cl]
            shifted = jnp.dot(sh_ref[...], x_cur.astype(BF16), preferred_element_type=F32)
            acc = cb_ref[:, cl] + cw_ref[tail:tail + 1, cl] * x_cur.astype(F32)
            prev8 = xp[:, cl]
            carry_in = jnp.zeros((SUBLANES, slab), F32)
            for j in range(1, CONV_WIDTH):
                wj = cw_ref[tail - j:tail - j + 1, cl]
                acc = acc + wj * shifted[(j - 1) * lc:j * lc]
                carry_in = carry_in + wj * jnp.where(row8 < j, pltpu.roll(prev8, j, 0), 0.0)
            acc = jnp.concatenate([acc[0:SUBLANES] + carry_in, acc[SUBLANES:]], axis=0)
        act = _silu(acc)
        if cs < SSD_WIDTH // slab:
            xs_scr[0:q, cl] = act
        else:
            bc_scr[0:q, cs * slab - SSD_WIDTH:(cs + 1) * slab - SSD_WIDTH] = act
    if decode:
        xs_scr[q:lc, :] = jnp.zeros((lc - q, SSD_WIDTH), F32)
        bc_scr[q:lc, :] = jnp.zeros((lc - q, 2 * SSD_GROUPS * D_STATE), F32)
        cout_ref[0] = xp[base + q - tail:base + q, :]
    else:
        xp[...] = xbc_ref[lc - 2 * SUBLANES:lc, :].astype(F32)[SUBLANES:2 * SUBLANES]

        @pl.when(chunk == pl.num_programs(1) - 1)
        def _():
            cout_ref[0] = xp[base - tail:base, :]

    dt_in = dt_ref[...]
    if q < lc:
        dt_in = jnp.concatenate([dt_in, jnp.zeros((lc - q, LANES), F32)], axis=0)
    xdt = dt_in + dtb_ref[...]
    dt = jnp.maximum(xdt, 0.0) + jnp.log1p(jnp.exp(-jnp.abs(xdt)))
    if q < lc:
        dt = jnp.where(lax.broadcasted_iota(jnp.int32, (lc, LANES), 0) < q, dt, 0.0)
    a = dt * (-jnp.exp(alog_ref[...]))
    a_hi = a.astype(BF16)
    a_r1 = a - a_hi.astype(F32)
    a_mid = a_r1.astype(BF16)
    a_lo = (a_r1 - a_mid.astype(F32)).astype(BF16)
    acs3 = jnp.dot(tri_ref[...], jnp.concatenate([a_hi, a_mid, a_lo], axis=1), preferred_element_type=F32)
    acs = acs3[:, 0:LANES] + acs3[:, LANES:2 * LANES] + acs3[:, 2 * LANES:3 * LANES]
    acs2 = acs * LOG2E
    acs_scr[...] = acs2
    acst_scr[...] = acs2.T
    dtt_scr[...] = dt.T

    def expand(v):
        hi, lo = _split2(v)
        return jnp.dot(jnp.concatenate([hi, lo], axis=1), e2_ref[...], preferred_element_type=F32)

    e_off = expand(jnp.exp(acs))
    w_state = expand(jnp.exp(acs[lc - 1:lc, :] - acs) * dt)
    chunk_decay = e_off[lc - 1:lc, :]

    causal = (lax.broadcasted_iota(jnp.int32, (q, lc), 0) >= lax.broadcasted_iota(jnp.int32, (q, lc), 1))
    lane_lo = lax.broadcasted_iota(jnp.int32, (1, LANES), 1) < HALF
    nbc = SSD_GROUPS * D_STATE
    pairs_per_group = SSD_HPG // 2
    gls = [slice(g * GROUP_WIDTH, (g + 1) * GROUP_WIDTH) for g in range(SSD_GROUPS)]
    b_gs = [bc_scr[:, g * D_STATE:(g + 1) * D_STATE] for g in range(SSD_GROUPS)]
    c_gs = [bc_scr[0:q, nbc + g * D_STATE:nbc + (g + 1) * D_STATE].astype(BF16) for g in range(SSD_GROUPS)]
    cbs = [lax.dot_general(c_gs[g], b_gs[g].astype(BF16), (((1,), (1,)), ((), ())), preferred_element_type=F32)
           for g in range(SSD_GROUPS)]
    y_offs = [jnp.dot(c_gs[g], ht[g].astype(BF16), preferred_element_type=F32) * e_off[0:q, gls[g]]
              for g in range(SSD_GROUPS)]
    for g in range(SSD_GROUPS):
        xw = (xs_scr[:, gls[g]] * w_state[:, gls[g]]).astype(BF16)
        ht[g] = ht[g] * chunk_decay[:, gls[g]] + jnp.dot(b_gs[g].T.astype(BF16), xw, preferred_element_type=F32)
    w_pairs = []
    for pair in range(SSD_HEADS // 2):
        w_parts = []
        for h in (2 * pair, 2 * pair + 1):
            seg2 = acs_scr[0:q, h:h + 1] - acst_scr[h:h + 1, :]
            decay = jnp.exp2(jnp.where(causal, seg2, -jnp.inf))
            w_parts.append(cbs[pair // pairs_per_group] * decay * dtt_scr[h:h + 1, :])
        w_pairs.append(jnp.concatenate(w_parts, axis=1).astype(BF16))
    y_diags = []
    for pair in range(SSD_HEADS // 2):
        x_pair = xs_scr[:, pair * LANES:(pair + 1) * LANES]
        x_bd = jnp.concatenate([jnp.where(lane_lo, x_pair, 0.0), jnp.where(lane_lo, 0.0, x_pair)],
                               axis=0).astype(BF16)
        y_diags.append(jnp.dot(w_pairs[pair], x_bd, preferred_element_type=F32))
    for pair in range(SSD_HEADS // 2):
        g, pr = divmod(pair, pairs_per_group)
        pl_ = slice(pair * LANES, (pair + 1) * LANES)
        y_pair = y_diags[pair] + y_offs[g][:, pr * LANES:(pr + 1) * LANES] + dsk_ref[:, pl_] * xs_scr[0:q, pl_]
        y_scr[:, pl_] = y_pair * _silu(zs_ref[:, pl_].astype(F32))
    for g in range(SSD_GROUPS):
        yg = y_scr[:, gls[g]]
        ms = jnp.mean(yg * yg, axis=-1, keepdims=True)
        os_ref[:, gls[g]] = (yg * lax.rsqrt(ms + EPS) * nw_ref[:, gls[g]]).astype(os_ref.dtype)

    def write_state():
        for g in range(SSD_GROUPS):
            hout_ref[0, g * GROUP_WIDTH:(g + 1) * GROUP_WIDTH, :] = ht[g].T

    if decode:
        write_state()
    else:
        pl.when(chunk == pl.num_programs(1) - 1)(write_state)


def _ssd(proj, dt_raw, consts, nseq, nchunk, q, decode, states, out_dtype):
    m = proj.shape[0]
    row_blk = lambda s, c: s * nchunk + c
    const = lambda s, c: (0, 0)
    in_specs = [
        pl.BlockSpec((q, CONV_DIM), lambda s, c: (row_blk(s, c), COL_XBC // CONV_DIM)),
        pl.BlockSpec((q, SSD_WIDTH), lambda s, c: (row_blk(s, c), COL_ZS // SSD_WIDTH)),
        pl.BlockSpec((q, LANES), lambda s, c: (row_blk(s, c), 0)),
        pl.BlockSpec((CONV_WIDTH, CONV_DIM), const),
        pl.BlockSpec((1, CONV_DIM), const),
        pl.BlockSpec((1, LANES), const),
        pl.BlockSpec((1, LANES), const),
        pl.BlockSpec((1, SSD_WIDTH), const),
        pl.BlockSpec((1, SSD_WIDTH), const),
        pl.BlockSpec((2 * LANES, SSD_WIDTH), const),
        pl.BlockSpec((SSD_CHUNK, SSD_CHUNK), const),
        pl.BlockSpec(((CONV_WIDTH - 1) * SSD_CHUNK, SSD_CHUNK), const),
    ]
    args = [proj, proj, dt_raw, *consts]
    if decode:
        in_specs += [
            pl.BlockSpec((1, CONV_WIDTH - 1, CONV_DIM), lambda s, c: (s, 0, 0)),
            pl.BlockSpec((1, SSD_WIDTH, D_STATE), lambda s, c: (s, 0, 0)),
        ]
        args += list(states)
    return pl.pallas_call(
        functools.partial(_ssd_body, q=q, decode=decode),
        grid=(nseq, nchunk),
        in_specs=in_specs,
        out_specs=[
            pl.BlockSpec((q, SSD_WIDTH), lambda s, c: (row_blk(s, c), 0)),
            pl.BlockSpec((1, CONV_WIDTH - 1, CONV_DIM), lambda s, c: (s, 0, 0)),
            pl.BlockSpec((1, SSD_WIDTH, D_STATE), lambda s, c: (s, 0, 0)),
        ],
        out_shape=[
            jax.ShapeDtypeStruct((m, SSD_WIDTH), out_dtype),
            jax.ShapeDtypeStruct((nseq, CONV_WIDTH - 1, CONV_DIM), F32),
            jax.ShapeDtypeStruct((nseq, SSD_WIDTH, D_STATE), F32),
        ],
        scratch_shapes=[
            pltpu.VMEM((SUBLANES + (q if decode else 0), CONV_DIM), F32),
            pltpu.VMEM((SSD_CHUNK, SSD_WIDTH), F32),
            pltpu.VMEM((SSD_CHUNK, 2 * SSD_GROUPS * D_STATE), F32),
            pltpu.VMEM((SSD_CHUNK, LANES), F32),
            pltpu.VMEM((LANES, SSD_CHUNK), F32),
            pltpu.VMEM((LANES, SSD_CHUNK), F32),
            pltpu.VMEM((SSD_GROUPS, D_STATE, GROUP_WIDTH), F32),
            pltpu.VMEM((q, SSD_WIDTH), F32),
        ],
        compiler_params=pltpu.CompilerParams(
            dimension_semantics=("arbitrary", "arbitrary"), vmem_limit_bytes=VMEM_LIMIT_BYTES),
        name="ssd_decode" if decode else "ssd_prompt",
    )(*args)


def _rope_tables(pos):
    half = ROPE_DIM // 2
    inv_freq = jnp.power(ROPE_THETA, -jnp.arange(half, dtype=F32) * (2.0 / ROPE_DIM))
    ang = pos.astype(F32)[:, None] * inv_freq[None, :]
    cos, sin = jnp.cos(ang), jnp.sin(ang)
    n = pos.shape[0]
    rest = HEAD_DIM - ROPE_DIM
    z_half, z_rest = jnp.zeros((n, half), F32), jnp.zeros((n, rest), F32)
    cos_t = jnp.concatenate([cos, cos, jnp.ones((n, rest), F32)], axis=1)
    sin_lo = jnp.concatenate([-sin, z_half, z_rest], axis=1)
    sin_hi = jnp.concatenate([z_half, sin, z_rest], axis=1)
    return tuple(jnp.tile(tb, (1, LANES // HEAD_DIM)) for tb in (cos_t, sin_lo, sin_hi))


def _pad_lanes(v):
    return jnp.pad(v.reshape(1, -1).astype(F32), ((0, 0), (0, LANES - v.shape[-1])))


def kernel(x_prompt, x_sample, cache_k, cache_v, state_conv, state_ssm, norm_w, w_in, q_norm_w, k_norm_w, sinks,
           conv_w, conv_b, dt_bias, A_log, D_skip, ssd_norm_w, w_attn_proj, w_ssd_proj, w_out):
    depth = norm_w.shape[0]
    assert depth == 1, "single-layer trunk"
    bp, seq = x_prompt.shape[:2]
    bs, tdec = x_sample.shape[:2]
    assert seq % WINDOW == 0 and tdec == SUBLANES and cache_k.shape[2] == WINDOW

    w = w_in[0]
    w_main = jnp.concatenate([
        w[:, OFF_ZA:OFF_ZS], w[:, OFF_DT:], w[:, :OFF_Q], w[:, OFF_V:OFF_ZA], w[:, OFF_ZS:OFF_XBC], w[:, OFF_Q:OFF_V],
    ], axis=1).astype(BF16)
    w_dt = jnp.pad(w[:, OFF_XBC:OFF_DT], ((0, 0), (0, LANES - SSD_HEADS))).astype(BF16)
    nw = norm_w[0].reshape(1, D_MODEL)
    w_ap, w_sp, w_o = w_attn_proj[0].astype(BF16), w_ssd_proj[0].astype(BF16), w_out[0].astype(BF16)

    lane = jnp.arange(LANES)
    bd = (lane[:, None] // HEAD_DIM == lane[None, :] // HEAD_DIM).astype(F32) / HEAD_DIM
    bd2 = jnp.concatenate([bd, bd], axis=0).astype(BF16)
    head_of_lane = jnp.arange(SSD_WIDTH) // SSD_HEAD_DIM
    e1 = (lane[:, None] == head_of_lane[None, :]).astype(BF16)
    e2 = jnp.concatenate([e1, e1], axis=0)
    tri = (lane[:, None] >= lane[None, :]).astype(BF16)
    shifts = jnp.concatenate([(lane[:, None] - j == lane[None, :]) for j in range(1, CONV_WIDTH)],
                             axis=0).astype(BF16)
    qnw2 = jnp.tile(q_norm_w[0].reshape(1, HEAD_DIM), (1, LANES // HEAD_DIM))
    knw2 = jnp.tile(k_norm_w[0].reshape(1, HEAD_DIM), (1, LANES // HEAD_DIM))
    ssd_consts = (conv_w[0], conv_b[0].reshape(1, CONV_DIM), _pad_lanes(dt_bias[0]), _pad_lanes(A_log[0]),
                  jnp.repeat(D_skip[0], SSD_HEAD_DIM).reshape(1, SSD_WIDTH), ssd_norm_w[0].reshape(1, SSD_WIDTH), e2, tri, shifts)
    sink_vec = sinks[0].astype(F32)

    xp2 = x_prompt.reshape(bp * seq, D_MODEL)
    nblk = seq // WINDOW
    proj_p, dt_p = _inproj(xp2, nw, w_main, w_dt, 512, BF16)
    tables_p = _rope_tables(jnp.arange(seq, dtype=jnp.int32))
    oa_p, kwin_p, vwin_p = _attention(proj_p, sink_vec, tables_p, qnw2, knw2, bd2, bp, nblk, 1, WINDOW, False, None, BF16)
    os_p, conv_p, ssm_p = _ssd(proj_p, dt_p, ssd_consts, bp, seq // SSD_CHUNK, SSD_CHUNK, False, None, BF16)
    y_p = _outproj(xp2, proj_p, oa_p, os_p, w_ap, w_sp, w_o, 512).reshape(bp, seq, D_MODEL)

    xs2 = x_sample.reshape(bs * tdec, D_MODEL)
    seqs_per_step = 4
    proj_s, dt_s = _inproj(xs2, nw, w_main, w_dt, 256, F32)
    pos_s = PAST_LEN + jnp.arange(tdec, dtype=jnp.int32)
    tables_s = tuple(jnp.tile(tb, (seqs_per_step, 1)) for tb in _rope_tables(pos_s))
    ck = cache_k[0].reshape(bs, WINDOW, KV_WIDTH)
    cv = cache_v[0].reshape(bs, WINDOW, KV_WIDTH)
    oa_s, kwin_s, vwin_s = _attention(proj_s, sink_vec, tables_s, qnw2, knw2, bd2, bs // seqs_per_step, 1,
                                      seqs_per_step, tdec, True, (ck, cv), F32)
    h0 = state_ssm[0].reshape(bs, SSD_WIDTH, D_STATE)
    os_s, conv_s, ssm_s = _ssd(proj_s, dt_s, ssd_consts, bs, 1, tdec, True, (state_conv[0], h0), F32)
    y_s = _outproj(xs2, proj_s, oa_s, os_s, w_ap, w_sp, w_o, 256).reshape(bs, tdec, D_MODEL)

    win = lambda a, n: a.reshape(1, n, WINDOW, N_KV_HEADS, HEAD_DIM)
    ssm = lambda a, n: a.reshape(1, n, SSD_HEADS, SSD_HEAD_DIM, D_STATE)
    return (y_p, y_s, win(kwin_p, bp), win(vwin_p, bp), conv_p[None], ssm(ssm_p, bp),
            win(kwin_s, bs), win(vwin_s, bs), conv_s[None], ssm(ssm_s, bs))
```

```python
import functools
import types

import jax
import jax.numpy as jnp
from jax import lax
from jax.experimental import pallas as pl
from jax.experimental.pallas import tpu as pltpu

F32 = jnp.float32
BF16 = jnp.bfloat16

D_MODEL = 1024
N_HEADS = 16
N_KV_HEADS = 4
HEAD_DIM = 64
KV_GROUP = N_HEADS // N_KV_HEADS
ATTN_WIDTH = N_HEADS * HEAD_DIM
KV_WIDTH = N_KV_HEADS * HEAD_DIM
WINDOW = 128
ROPE_DIM = HEAD_DIM // 4
ROPE_THETA = 500000.0
PAST_LEN = 16384
SSD_WIDTH = 2 * D_MODEL
SSD_HEAD_DIM = 64
SSD_HEADS = SSD_WIDTH // SSD_HEAD_DIM
SSD_GROUPS = 4
SSD_HPG = SSD_HEADS // SSD_GROUPS
D_STATE = 128
CONV_WIDTH = 4
CONV_DIM = SSD_WIDTH + 2 * SSD_GROUPS * D_STATE
SSD_CHUNK = 128
EPS = 1e-6

OFF_Q = ATTN_WIDTH
OFF_K = OFF_Q + KV_WIDTH
OFF_V = OFF_K + KV_WIDTH
OFF_ZA = OFF_V + ATTN_WIDTH
OFF_ZS = OFF_ZA + SSD_WIDTH
OFF_XBC = OFF_ZS + CONV_DIM
OFF_DT = OFF_XBC + SSD_HEADS

LANES = 128
SUBLANES = 8
VMEM_LIMIT_BYTES = 56 * 1024 * 1024

COL_ZS = 0
COL_G = COL_ZS + SSD_WIDTH
COL_Q = COL_G + 2 * D_MODEL
COL_ZA = COL_Q + ATTN_WIDTH
COL_XBC = COL_ZA + ATTN_WIDTH
COL_KV = COL_XBC + CONV_DIM
N_MAIN = COL_KV + 2 * KV_WIDTH
PROJ_TILE_N = 512
TRUNK_TILE = 2 * SSD_CHUNK
GROUP_WIDTH = SSD_WIDTH // SSD_GROUPS
HALF = LANES // 2
NEG_BIG = -1e30
LOG2E = 1.4426950408889634


def _sigmoid(x):
    return 0.5 * jnp.tanh(0.5 * x) + 0.5


def _silu(x):
    hx = 0.5 * x
    return hx * jnp.tanh(hx) + hx


def _split2(v):
    hi = v.astype(BF16)
    lo = (v - hi.astype(F32)).astype(BF16)
    return hi, lo


def _drain(gen):
    for _ in gen:
        pass


def _interleave(main, side, every):
    live = {"main": True, "side": True}

    def step(gen, key):
        if live[key]:
            try:
                next(gen)
            except StopIteration:
                live[key] = False

    while live["main"] or live["side"]:
        for _ in range(every):
            step(main, "main")
        step(side, "side")


def _inproj_gen(x_ref, nw_ref, w_ref, wdt_ref, store_cols, store_dt, h_scr):
    x = x_ref[...]
    ms = jnp.mean(x * x, axis=-1, keepdims=True)
    h_scr[...] = (x * lax.rsqrt(ms + EPS) * nw_ref[...]).astype(BF16)
    yield
    for j in range(N_MAIN // PROJ_TILE_N):
        sl = slice(j * PROJ_TILE_N, (j + 1) * PROJ_TILE_N)
        store_cols(j, jnp.dot(h_scr[...], w_ref[:, sl], preferred_element_type=F32))
        yield
    store_dt(jnp.dot(h_scr[...], wdt_ref[...], preferred_element_type=F32))
    yield


def _inproj_body(x_ref, nw_ref, w_ref, wdt_ref, out_ref, dt_ref, h_scr):
    def store_cols(j, val):
        out_ref[:, j * PROJ_TILE_N:(j + 1) * PROJ_TILE_N] = val.astype(out_ref.dtype)

    def store_dt(val):
        dt_ref[...] = val

    _drain(_inproj_gen(x_ref, nw_ref, w_ref, wdt_ref, store_cols, store_dt, h_scr))


def _inproj(x2d, norm_w, w_main, w_dt, tm, out_dtype):
    m = x2d.shape[0]
    return pl.pallas_call(
        _inproj_body,
        grid=(m // tm,),
        in_specs=[
            pl.BlockSpec((tm, D_MODEL), lambda i: (i, 0)),
            pl.BlockSpec((1, D_MODEL), lambda i: (0, 0)),
            pl.BlockSpec((D_MODEL, N_MAIN), lambda i: (0, 0), pipeline_mode=pl.Buffered(1)),
            pl.BlockSpec((D_MODEL, LANES), lambda i: (0, 0), pipeline_mode=pl.Buffered(1)),
        ],
        out_specs=[
            pl.BlockSpec((tm, N_MAIN), lambda i: (i, 0)),
            pl.BlockSpec((tm, LANES), lambda i: (i, 0)),
        ],
        out_shape=[
            jax.ShapeDtypeStruct((m, N_MAIN), out_dtype),
            jax.ShapeDtypeStruct((m, LANES), F32),
        ],
        scratch_shapes=[pltpu.VMEM((tm, D_MODEL), BF16)],
        compiler_params=pltpu.CompilerParams(
            dimension_semantics=("arbitrary",), vmem_limit_bytes=VMEM_LIMIT_BYTES),
        name="inproj",
    )(x2d, norm_w, w_main, w_dt)


def _outproj_body(x_ref, g_ref, oa_ref, os_ref, wa_ref, ws_ref, wo_ref, y_ref):
    pa = jnp.dot(oa_ref[...].astype(BF16), wa_ref[...], preferred_element_type=F32)
    ps = jnp.dot(os_ref[...].astype(BF16), ws_ref[...], preferred_element_type=F32)
    g = g_ref[...].astype(F32)
    merged = _sigmoid(g[:, :D_MODEL]) * pa + _sigmoid(g[:, D_MODEL:]) * ps
    y_ref[...] = x_ref[...] + jnp.dot(merged.astype(BF16), wo_ref[...], preferred_element_type=F32)


def _outproj(x2d, gates, gate_col_block, o_a, o_s, w_ap, w_sp, w_o, tm):
    m = x2d.shape[0]
    const = lambda i: (0, 0)
    return pl.pallas_call(
        _outproj_body,
        grid=(m // tm,),
        in_specs=[
            pl.BlockSpec((tm, D_MODEL), lambda i: (i, 0)),
            pl.BlockSpec((tm, 2 * D_MODEL), lambda i: (i, gate_col_block)),
            pl.BlockSpec((tm, ATTN_WIDTH), lambda i: (i, 0)),
            pl.BlockSpec((tm, SSD_WIDTH), lambda i: (i, 0)),
            pl.BlockSpec((ATTN_WIDTH, D_MODEL), const, pipeline_mode=pl.Buffered(1)),
            pl.BlockSpec((SSD_WIDTH, D_MODEL), const, pipeline_mode=pl.Buffered(1)),
            pl.BlockSpec((D_MODEL, D_MODEL), const, pipeline_mode=pl.Buffered(1)),
        ],
        out_specs=pl.BlockSpec((tm, D_MODEL), lambda i: (i, 0)),
        out_shape=jax.ShapeDtypeStruct((m, D_MODEL), F32),
        compiler_params=pltpu.CompilerParams(
            dimension_semantics=("arbitrary",), vmem_limit_bytes=VMEM_LIMIT_BYTES),
        name="outproj",
    )(x2d, gates, o_a, o_s, w_ap, w_sp, w_o)


def _norm_rope(xc, nw, bd2, cos_t, sin_lo, sin_hi):
    hi, lo = _split2(xc * xc)
    ms = jnp.dot(jnp.concatenate([hi, lo], axis=1), bd2, preferred_element_type=F32)
    xn = xc * lax.rsqrt(ms + EPS) * nw
    return xn * cos_t + pltpu.roll(xn, LANES - ROPE_DIM // 2, 1) * sin_lo + pltpu.roll(xn, ROPE_DIM // 2, 1) * sin_hi


def _dup_half(chunk, keep_low, lane_lo):
    rolled = pltpu.roll(chunk, HALF, 1)
    return jnp.where(lane_lo, chunk, rolled) if keep_low else jnp.where(lane_lo, rolled, chunk)


def _attn_reset(kdup, vdup):
    for g in range(N_KV_HEADS):
        kdup[0, g, WINDOW:2 * WINDOW, :] = jnp.zeros((WINDOW, LANES), BF16)
        vdup[0, g, WINDOW:2 * WINDOW, 0:LANES] = jnp.zeros((WINDOW, LANES), BF16)
    vdup[:, :, :, LANES:2 * LANES] = jnp.ones((vdup.shape[0], N_KV_HEADS, 2 * WINDOW, LANES), BF16)


def _attn_gen(io, sinks_ref, qnw_ref, knw_ref, bd2_ref, kdup, vdup, o_scr, *, seqs, t, decode, prev_valid,
              own_half):
    tp = max(t, 2 * SUBLANES)
    nkeys = 2 * WINDOW
    own_lo, prev_lo = own_half * WINDOW, (1 - own_half) * WINDOW
    bd2 = bd2_ref[...]
    cos_t, sin_lo, sin_hi = io.tables()
    lane_lo = lax.broadcasted_iota(jnp.int32, (1, LANES), 1) < HALF

    kv = io.kv().astype(F32)
    k_rot = [_norm_rope(kv[:, c * LANES:(c + 1) * LANES], knw_ref[...], bd2, cos_t, sin_lo, sin_hi)
             for c in range(KV_WIDTH // LANES)]
    v_raw = [kv[:, KV_WIDTH + c * LANES:KV_WIDTH + (c + 1) * LANES] for c in range(KV_WIDTH // LANES)]
    yield
    q_all = io.q().astype(F32)
    q_rot = []
    for c in range(ATTN_WIDTH // LANES):
        q_rot.append(_norm_rope(q_all[:, c * LANES:(c + 1) * LANES], qnw_ref[...], bd2, cos_t, sin_lo, sin_hi)
                     * (HEAD_DIM ** -0.5))
        if c % 2 == 1:
            yield

    if decode:
        vdup[:, :, :, LANES:2 * LANES] = jnp.ones((seqs, N_KV_HEADS, nkeys, LANES), BF16)

    own = (lax.broadcasted_iota(jnp.int32, (t, WINDOW), 1) <= lax.broadcasted_iota(jnp.int32, (t, WINDOW), 0))
    prev_bias = None if prev_valid is None else jnp.where(prev_valid, 0.0, NEG_BIG)

    def pad_rows(a):
        if tp == t:
            return a
        return jnp.concatenate([a, jnp.zeros((tp - t, a.shape[1]), a.dtype)], axis=0)

    for si in range(seqs):
        rs = slice(si * t, (si + 1) * t)
        for g in range(N_KV_HEADS):
            c, low = g // 2, (g % 2 == 0)
            kdup[si, g, own_lo:own_lo + tp, :] = pad_rows(_dup_half(k_rot[c][rs], low, lane_lo)).astype(BF16)
            vdup[si, g, own_lo:own_lo + tp, 0:LANES] = pad_rows(_dup_half(v_raw[c][rs], low, lane_lo)).astype(BF16)
            if decode:
                kdup[si, g, prev_lo:prev_lo + WINDOW, :] = _dup_half(
                    io.ck(si)[:, c * LANES:(c + 1) * LANES], low, lane_lo).astype(BF16)
                vdup[si, g, prev_lo:prev_lo + WINDOW, 0:LANES] = _dup_half(
                    io.cv(si)[:, c * LANES:(c + 1) * LANES], low, lane_lo).astype(BF16)
                kdup[si, g, own_lo + tp:own_lo + WINDOW, :] = jnp.zeros((WINDOW - tp, LANES), BF16)
                vdup[si, g, own_lo + tp:own_lo + WINDOW, 0:LANES] = jnp.zeros((WINDOW - tp, LANES), BF16)
        yield
        scores = []
        for g in range(N_KV_HEADS):
            q_stack = jnp.concatenate(
                [jnp.where(lane_lo if j % 2 == 0 else jnp.logical_not(lane_lo), q_rot[j // 2][rs], 0.0)
                 for j in range(KV_GROUP * g, KV_GROUP * (g + 1))], axis=0).astype(BF16)
            scores.append(lax.dot_general(q_stack, kdup[si, g], (((1,), (1,)), ((), ())),
                                          preferred_element_type=F32))
            yield
        probs, sink_terms = [], []
        for j in range(N_HEADS):
            s = scores[j // KV_GROUP]
            r0 = (j % KV_GROUP) * t
            s_prev = s[r0:r0 + t, prev_lo:prev_lo + WINDOW]
            if prev_bias is not None:
                s_prev = s_prev + prev_bias
            sj = jnp.where(own, s[r0:r0 + t, own_lo:own_lo + WINDOW], s_prev)
            sink = sinks_ref[j]
            mx = jnp.maximum(jnp.max(sj, axis=-1, keepdims=True), sink)
            p = jnp.exp(sj - mx)
            sink_terms.append(jnp.exp(sink - mx))
            halves = [jnp.where(own, 0.0, p), jnp.where(own, p, 0.0)]
            probs.append(jnp.concatenate(halves if own_half == 1 else halves[::-1], axis=1))
            if j % KV_GROUP == KV_GROUP - 1:
                yield
        for g in range(N_KV_HEADS):
            p_all = jnp.concatenate(probs[KV_GROUP * g:KV_GROUP * (g + 1)], axis=0).astype(BF16)
            o_aug = jnp.dot(p_all, vdup[si, g], preferred_element_type=F32)
            o = [o_aug[jj * t:(jj + 1) * t, 0:LANES]
                 / (o_aug[jj * t:(jj + 1) * t, LANES:2 * LANES] + sink_terms[KV_GROUP * g + jj])
                 for jj in range(KV_GROUP)]
            for half in range(2):
                o_scr[rs, (2 * g + half) * LANES:(2 * g + half + 1) * LANES] = jnp.where(
                    lane_lo, o[2 * half], o[2 * half + 1])
            yield

    io.oa(o_scr[...] * _silu(io.za().astype(F32)))
    yield

    k_new = jnp.concatenate(k_rot, axis=1)
    v_new = jnp.concatenate(v_raw, axis=1)
    if decode:
        for si in range(seqs):
            io.kwin(si, 0, WINDOW - t, io.ck(si)[t:WINDOW, :])
            io.vwin(si, 0, WINDOW - t, io.cv(si)[t:WINDOW, :])
            io.kwin(si, WINDOW - t, WINDOW, k_new[si * t:(si + 1) * t])
            io.vwin(si, WINDOW - t, WINDOW, v_new[si * t:(si + 1) * t])
    else:
        io.kwin(0, 0, WINDOW, k_new)
        io.vwin(0, 0, WINDOW, v_new)
    yield


def _attn_scratch(seqs, rows):
    return [
        pltpu.VMEM((seqs, N_KV_HEADS, 2 * WINDOW, LANES), BF16),
        pltpu.VMEM((seqs, N_KV_HEADS, 2 * WINDOW, 2 * LANES), BF16),
        pltpu.VMEM((rows, ATTN_WIDTH), F32),
    ]


def _attn_decode_body(sinks_ref, q_ref, kv_ref, za_ref, cos_ref, slo_ref, shi_ref, qnw_ref, knw_ref, bd2_ref,
                      ck_ref, cv_ref, oa_ref, kwin_ref, vwin_ref, kdup, vdup, o_scr, *, seqs, t):
    def store_oa(val):
        oa_ref[...] = val.astype(oa_ref.dtype)

    def store_kwin(si, lo, hi, val):
        kwin_ref[si, lo:hi, :] = val

    def store_vwin(si, lo, hi, val):
        vwin_ref[si, lo:hi, :] = val

    io = types.SimpleNamespace(
        q=lambda: q_ref[...], kv=lambda: kv_ref[...], za=lambda: za_ref[...],
        tables=lambda: (cos_ref[...], slo_ref[...], shi_ref[...]),
        oa=store_oa, kwin=store_kwin, vwin=store_vwin, ck=lambda si: ck_ref[si], cv=lambda si: cv_ref[si])
    _drain(_attn_gen(io, sinks_ref, qnw_ref, knw_ref, bd2_ref, kdup, vdup, o_scr,
                     seqs=seqs, t=t, decode=True, prev_valid=None, own_half=1))


def _attention_decode(proj, sinks, tables, qnw2, knw2, bd2, caches, nsteps, seqs, t):
    rows = seqs * t
    m = proj.shape[0]
    const = lambda s: (0, 0)
    in_specs = [
        pl.BlockSpec(memory_space=pltpu.SMEM),
        pl.BlockSpec((rows, ATTN_WIDTH), lambda s: (s, COL_Q // ATTN_WIDTH)),
        pl.BlockSpec((rows, 2 * KV_WIDTH), lambda s: (s, COL_KV // (2 * KV_WIDTH))),
        pl.BlockSpec((rows, ATTN_WIDTH), lambda s: (s, COL_ZA // ATTN_WIDTH)),
        pl.BlockSpec((rows, LANES), const),
        pl.BlockSpec((rows, LANES), const),
        pl.BlockSpec((rows, LANES), const),
        pl.BlockSpec((1, LANES), const),
        pl.BlockSpec((1, LANES), const),
        pl.BlockSpec((2 * LANES, LANES), const),
        pl.BlockSpec((seqs, WINDOW, KV_WIDTH), lambda s: (s, 0, 0)),
        pl.BlockSpec((seqs, WINDOW, KV_WIDTH), lambda s: (s, 0, 0)),
    ]
    return pl.pallas_call(
        functools.partial(_attn_decode_body, seqs=seqs, t=t),
        grid=(nsteps,),
        in_specs=in_specs,
        out_specs=[
            pl.BlockSpec((rows, ATTN_WIDTH), lambda s: (s, 0)),
            pl.BlockSpec((seqs, WINDOW, KV_WIDTH), lambda s: (s, 0, 0)),
            pl.BlockSpec((seqs, WINDOW, KV_WIDTH), lambda s: (s, 0, 0)),
        ],
        out_shape=[
            jax.ShapeDtypeStruct((m, ATTN_WIDTH), F32),
            jax.ShapeDtypeStruct((nsteps * seqs, WINDOW, KV_WIDTH), F32),
            jax.ShapeDtypeStruct((nsteps * seqs, WINDOW, KV_WIDTH), F32),
        ],
        scratch_shapes=_attn_scratch(seqs, rows),
        compiler_params=pltpu.CompilerParams(
            dimension_semantics=("arbitrary",), vmem_limit_bytes=VMEM_LIMIT_BYTES),
        name="attn_decode",
    )(sinks, proj, proj, proj, *tables, qnw2, knw2, bd2, *caches)


def _ssd_gen(io, cw_ref, cb_ref, dtb_ref, alog_ref, dsk_ref, nw_ref, e2_ref, tri_ref, sh_ref,
             xp, xs_scr, bc_scr, acs_scr, acst_scr, dtt_scr, ht, y_scr, *, q, decode):
    lc = SSD_CHUNK
    tail = CONV_WIDTH - 1
    base = SUBLANES

    slab = 4 * LANES
    row8 = lax.broadcasted_iota(jnp.int32, (SUBLANES, 1), 0)
    for cs in range(CONV_DIM // slab):
        cl = slice(cs * slab, (cs + 1) * slab)
        if decode:
            acc = cb_ref[:, cl] + cw_ref[0:1, cl] * xp[base - tail:base - tail + q, cl]
            for j in range(1, CONV_WIDTH):
                acc = acc + cw_ref[j:j + 1, cl] * xp[base - tail + j:base - tail + j + q, cl]
        else:
            x_cur = io.xbc(cl)
            shifted = jnp.dot(sh_ref[...], x_cur.astype(BF16), preferred_element_type=F32)
            acc = cb_ref[:, cl] + cw_ref[tail:tail + 1, cl] * x_cur.astype(F32)
            prev8 = xp[:, cl]
            carry_in = jnp.zeros((SUBLANES, slab), F32)
            for j in range(1, CONV_WIDTH):
                wj = cw_ref[tail - j:tail - j + 1, cl]
                acc = acc + wj * shifted[(j - 1) * lc:j * lc]
                carry_in = carry_in + wj * jnp.where(row8 < j, pltpu.roll(prev8, j, 0), 0.0)
            acc = jnp.concatenate([acc[0:SUBLANES] + carry_in, acc[SUBLANES:]], axis=0)
        act = _silu(acc)
        if cs < SSD_WIDTH // slab:
            xs_scr[0:q, cl] = act
        else:
            bc_scr[0:q, cs * slab - SSD_WIDTH:(cs + 1) * slab - SSD_WIDTH] = act
        yield
    if decode:
        xs_scr[q:lc, :] = jnp.zeros((lc - q, SSD_WIDTH), F32)
        bc_scr[q:lc, :] = jnp.zeros((lc - q, 2 * SSD_GROUPS * D_STATE), F32)
    else:
        xp[...] = io.xbc_last_rows().astype(F32)[SUBLANES:2 * SUBLANES]

    dt_in = io.dt()
    if q < lc:
        dt_in = jnp.concatenate([dt_in, jnp.zeros((lc - q, LANES), F32)], axis=0)
    xdt = dt_in + dtb_ref[...]
    dt = jnp.maximum(xdt, 0.0) + jnp.log1p(jnp.exp(-jnp.abs(xdt)))
    if q < lc:
        dt = jnp.where(lax.broadcasted_iota(jnp.int32, (lc, LANES), 0) < q, dt, 0.0)
    a = dt * (-jnp.exp(alog_ref[...]))
    a_hi = a.astype(BF16)
    a_r1 = a - a_hi.astype(F32)
    a_mid = a_r1.astype(BF16)
    a_lo = (a_r1 - a_mid.astype(F32)).astype(BF16)
    acs3 = jnp.dot(tri_ref[...], jnp.concatenate([a_hi, a_mid, a_lo], axis=1), preferred_element_type=F32)
    acs = acs3[:, 0:LANES] + acs3[:, LANES:2 * LANES] + acs3[:, 2 * LANES:3 * LANES]
    acs2 = acs * LOG2E
    acs_scr[...] = acs2
    acst_scr[...] = acs2.T
    dtt_scr[...] = dt.T
    yield

    def expand(v):
        hi, lo = _split2(v)
        return jnp.dot(jnp.concatenate([hi, lo], axis=1), e2_ref[...], preferred_element_type=F32)

    e_off = expand(jnp.exp(acs))
    yield
    w_state = expand(jnp.exp(acs[lc - 1:lc, :] - acs) * dt)
    chunk_decay = e_off[lc - 1:lc, :]
    yield

    causal = (lax.broadcasted_iota(jnp.int32, (q, lc), 0) >= lax.broadcasted_iota(jnp.int32, (q, lc), 1))
    lane_lo = lax.broadcasted_iota(jnp.int32, (1, LANES), 1) < HALF
    nbc = SSD_GROUPS * D_STATE
    pairs_per_group = SSD_HPG // 2
    gls = [slice(g * GROUP_WIDTH, (g + 1) * GROUP_WIDTH) for g in range(SSD_GROUPS)]
    b_gs = [bc_scr[:, g * D_STATE:(g + 1) * D_STATE] for g in range(SSD_GROUPS)]
    c_gs = [bc_scr[0:q, nbc + g * D_STATE:nbc + (g + 1) * D_STATE].astype(BF16) for g in range(SSD_GROUPS)]
    cbs = [lax.dot_general(c_gs[g], b_gs[g].astype(BF16), (((1,), (1,)), ((), ())), preferred_element_type=F32)
           for g in range(SSD_GROUPS)]
    yield
    y_offs = []
    for g in range(SSD_GROUPS):
        y_offs.append(jnp.dot(c_gs[g], ht[g].astype(BF16), preferred_element_type=F32) * e_off[0:q, gls[g]])
        yield
    for g in range(SSD_GROUPS):
        xw = (xs_scr[:, gls[g]] * w_state[:, gls[g]]).astype(BF16)
        ht[g] = ht[g] * chunk_decay[:, gls[g]] + jnp.dot(b_gs[g].T.astype(BF16), xw, preferred_element_type=F32)
        yield
    w_pairs = []
    for pair in range(SSD_HEADS // 2):
        w_parts = []
        for h in (2 * pair, 2 * pair + 1):
            seg2 = acs_scr[0:q, h:h + 1] - acst_scr[h:h + 1, :]
            decay = jnp.exp2(jnp.where(causal, seg2, -jnp.inf))
            w_parts.append(cbs[pair // pairs_per_group] * decay * dtt_scr[h:h + 1, :])
        w_pairs.append(jnp.concatenate(w_parts, axis=1).astype(BF16))
        if pair % 2 == 1:
            yield
    y_diags = []
    for pair in range(SSD_HEADS // 2):
        x_pair = xs_scr[:, pair * LANES:(pair + 1) * LANES]
        x_bd = jnp.concatenate([jnp.where(lane_lo, x_pair, 0.0), jnp.where(lane_lo, 0.0, x_pair)],
                               axis=0).astype(BF16)
        y_diags.append(jnp.dot(w_pairs[pair], x_bd, preferred_element_type=F32))
        if pair % 4 == 3:
            yield
    for pair in range(SSD_HEADS // 2):
        g, pr = divmod(pair, pairs_per_group)
        pl_ = slice(pair * LANES, (pair + 1) * LANES)
        y_pair = y_diags[pair] + y_offs[g][:, pr * LANES:(pr + 1) * LANES] + dsk_ref[:, pl_] * xs_scr[0:q, pl_]
        y_scr[:, pl_] = y_pair * _silu(io.zs(pl_).astype(F32))
        if pair % 4 == 3:
            yield
    for g in range(SSD_GROUPS):
        yg = y_scr[:, gls[g]]
        ms = jnp.mean(yg * yg, axis=-1, keepdims=True)
        io.os(gls[g], yg * lax.rsqrt(ms + EPS) * nw_ref[:, gls[g]])
        yield


def _ssd_write_state(hout_ref, ht):
    for g in range(SSD_GROUPS):
        hout_ref[0, g * GROUP_WIDTH:(g + 1) * GROUP_WIDTH, :] = ht[g].T


def _ssd_scratch(q, staged_rows):
    return [
        pltpu.VMEM((SUBLANES + staged_rows, CONV_DIM), F32),
        pltpu.VMEM((SSD_CHUNK, SSD_WIDTH), F32),
        pltpu.VMEM((SSD_CHUNK, 2 * SSD_GROUPS * D_STATE), F32),
        pltpu.VMEM((SSD_CHUNK, LANES), F32),
        pltpu.VMEM((LANES, SSD_CHUNK), F32),
        pltpu.VMEM((LANES, SSD_CHUNK), F32),
        pltpu.VMEM((SSD_GROUPS, D_STATE, GROUP_WIDTH), F32),
        pltpu.VMEM((q, SSD_WIDTH), F32),
    ]


def _ssd_decode_body(xbc_ref, zs_ref, dt_ref, cw_ref, cb_ref, dtb_ref, alog_ref, dsk_ref, nw_ref, e2_ref, tri_ref,
                     sh_ref, cst_ref, h0_ref, os_ref, cout_ref, hout_ref,
                     xp, xs_scr, bc_scr, acs_scr, acst_scr, dtt_scr, ht, y_scr, *, q):
    tail = CONV_WIDTH - 1
    base = SUBLANES
    xp[base - tail:base, :] = cst_ref[0]
    xp[base:base + q, :] = xbc_ref[...].astype(F32)
    for g in range(SSD_GROUPS):
        ht[g] = h0_ref[0, g * GROUP_WIDTH:(g + 1) * GROUP_WIDTH, :].T

    def store_os(cols, val):
        os_ref[:, cols] = val.astype(os_ref.dtype)

    io = types.SimpleNamespace(zs=lambda cols: zs_ref[:, cols], dt=lambda: dt_ref[...], os=store_os)
    _drain(_ssd_gen(io, cw_ref, cb_ref, dtb_ref, alog_ref, dsk_ref, nw_ref, e2_ref, tri_ref, sh_ref,
                    xp, xs_scr, bc_scr, acs_scr, acst_scr, dtt_scr, ht, y_scr, q=q, decode=True))
    cout_ref[0] = xp[base + q - tail:base + q, :]
    _ssd_write_state(hout_ref, ht)


def _ssd_const_specs(const):
    return [
        pl.BlockSpec((CONV_WIDTH, CONV_DIM), const),
        pl.BlockSpec((1, CONV_DIM), const),
        pl.BlockSpec((1, LANES), const),
        pl.BlockSpec((1, LANES), const),
        pl.BlockSpec((1, SSD_WIDTH), const),
        pl.BlockSpec((1, SSD_WIDTH), const),
        pl.BlockSpec((2 * LANES, SSD_WIDTH), const),
        pl.BlockSpec((SSD_CHUNK, SSD_CHUNK), const),
        pl.BlockSpec(((CONV_WIDTH - 1) * SSD_CHUNK, SSD_CHUNK), const),
    ]


def _ssd_decode(proj, dt_raw, consts, states, nseq, q):
    m = proj.shape[0]
    const = lambda s: (0, 0)
    in_specs = [
        pl.BlockSpec((q, CONV_DIM), lambda s: (s, COL_XBC // CONV_DIM)),
        pl.BlockSpec((q, SSD_WIDTH), lambda s: (s, COL_ZS // SSD_WIDTH)),
        pl.BlockSpec((q, LANES), lambda s: (s, 0)),
        *_ssd_const_specs(const),
        pl.BlockSpec((1, CONV_WIDTH - 1, CONV_DIM), lambda s: (s, 0, 0)),
        pl.BlockSpec((1, SSD_WIDTH, D_STATE), lambda s: (s, 0, 0)),
    ]
    return pl.pallas_call(
        functools.partial(_ssd_decode_body, q=q),
        grid=(nseq,),
        in_specs=in_specs,
        out_specs=[
            pl.BlockSpec((q, SSD_WIDTH), lambda s: (s, 0)),
            pl.BlockSpec((1, CONV_WIDTH - 1, CONV_DIM), lambda s: (s, 0, 0)),
            pl.BlockSpec((1, SSD_WIDTH, D_STATE), lambda s: (s, 0, 0)),
        ],
        out_shape=[
            jax.ShapeDtypeStruct((m, SSD_WIDTH), F32),
            jax.ShapeDtypeStruct((nseq, CONV_WIDTH - 1, CONV_DIM), F32),
            jax.ShapeDtypeStruct((nseq, SSD_WIDTH, D_STATE), F32),
        ],
        scratch_shapes=_ssd_scratch(q, q),
        compiler_params=pltpu.CompilerParams(
            dimension_semantics=("arbitrary",), vmem_limit_bytes=VMEM_LIMIT_BYTES),
        name="ssd_decode",
    )(proj, proj, dt_raw, *consts, *states)


def _trunk_body(sinks_ref, x_ref, nw_ref, w_ref, wdt_ref, cos_ref, slo_ref, shi_ref, qnw_ref, knw_ref, bd2_ref,
                cw_ref, cb_ref, dtb_ref, alog_ref, dsk_ref, snw_ref, e2_ref, tri_ref, sh_ref,
                g_ref, oa_ref, os_ref, kwin_ref, vwin_ref, cout_ref, hout_ref,
                proj_a, proj_b, dt_a, dt_b, h_scr, kdup, vdup, o_scr,
                xp, xs_scr, bc_scr, acs_scr, acst_scr, dtt_scr, ht, y_scr, *, tiles_per_seq):
    j = pl.program_id(0)
    tile_in_seq = (j + tiles_per_seq - 1) % tiles_per_seq
    seq_start = tile_in_seq == 0
    chunks = TRUNK_TILE // SSD_CHUNK
    assert chunks % 2 == 0, "chunk parity picks the key/value scratch half across tiles"
    g_tiles = range(COL_G // PROJ_TILE_N, COL_Q // PROJ_TILE_N)

    @pl.when(j == 0)
    def _():
        proj_b[...] = jnp.zeros(proj_b.shape, proj_b.dtype)
        dt_b[...] = jnp.zeros(dt_b.shape, F32)

    @pl.when(seq_start | (j == 0))
    def _():
        _attn_reset(kdup, vdup)
        xp[...] = jnp.zeros(xp.shape, F32)
        ht[...] = jnp.zeros(ht.shape, F32)

    def step(wbuf, wdt, rbuf, rdt):
        def store_cols(jn, val):
            if jn in g_tiles:
                g_ref[:, (jn - g_tiles[0]) * PROJ_TILE_N:(jn - g_tiles[0] + 1) * PROJ_TILE_N] = val.astype(g_ref.dtype)
            else:
                wbuf[:, jn * PROJ_TILE_N:(jn + 1) * PROJ_TILE_N] = val.astype(wbuf.dtype)

        def store_dt(val):
            wdt[...] = val

        producer = _inproj_gen(x_ref, nw_ref, w_ref, wdt_ref, store_cols, store_dt, h_scr)

        def consumer():
            for c in range(chunks):
                rows = slice(c * SSD_CHUNK, (c + 1) * SSD_CHUNK)

                def store_oa(val, rows=rows):
                    oa_ref[rows, :] = val.astype(oa_ref.dtype)

                def store_kwin(si, lo, hi, val):
                    kwin_ref[0, lo:hi, :] = val

                def store_vwin(si, lo, hi, val):
                    vwin_ref[0, lo:hi, :] = val

                def store_os(cols, val, rows=rows):
                    os_ref[rows, cols] = val.astype(os_ref.dtype)

                attn_io = types.SimpleNamespace(
                    q=lambda rows=rows: rbuf[rows, COL_Q:COL_Q + ATTN_WIDTH],
                    kv=lambda rows=rows: rbuf[rows, COL_KV:COL_KV + 2 * KV_WIDTH],
                    za=lambda rows=rows: rbuf[rows, COL_ZA:COL_ZA + ATTN_WIDTH],
                    tables=lambda rows=rows: (cos_ref[rows, :], slo_ref[rows, :], shi_ref[rows, :]),
                    oa=store_oa, kwin=store_kwin, vwin=store_vwin)
                ssd_io = types.SimpleNamespace(
                    xbc=lambda cols, rows=rows: rbuf[rows, COL_XBC + cols.start:COL_XBC + cols.stop],
                    xbc_last_rows=lambda c=c: rbuf[(c + 1) * SSD_CHUNK - 2 * SUBLANES:(c + 1) * SSD_CHUNK,
                                                   COL_XBC:COL_XBC + CONV_DIM],
                    zs=lambda cols, rows=rows: rbuf[rows, COL_ZS + cols.start:COL_ZS + cols.stop],
                    dt=lambda rows=rows: rdt[rows, :], os=store_os)
                prev_valid = (tile_in_seq > 0) if c == 0 else True
                attn = _attn_gen(attn_io, sinks_ref, qnw_ref, knw_ref, bd2_ref, kdup, vdup, o_scr,
                                 seqs=1, t=WINDOW, decode=False, prev_valid=prev_valid, own_half=c % 2)
                ssd = _ssd_gen(ssd_io, cw_ref, cb_ref, dtb_ref, alog_ref, dsk_ref, snw_ref, e2_ref, tri_ref, sh_ref,
                               xp, xs_scr, bc_scr, acs_scr, acst_scr, dtt_scr, ht, y_scr, q=SSD_CHUNK, decode=False)
                live = [attn, ssd]
                while live:
                    for gen in list(live):
                        try:
                            next(gen)
                        except StopIteration:
                            live.remove(gen)
                        yield

        _interleave(consumer(), producer, every=4)

    pl.when(j % 2 == 0)(lambda: step(proj_a, dt_a, proj_b, dt_b))
    pl.when(j % 2 == 1)(lambda: step(proj_b, dt_b, proj_a, dt_a))

    @pl.when((tile_in_seq == tiles_per_seq - 1) & (j > 0))
    def _():
        cout_ref[0] = xp[SUBLANES - (CONV_WIDTH - 1):SUBLANES, :]
        _ssd_write_state(hout_ref, ht)


def _trunk(x2d, sinks, nw, w_main, w_dt, tables, qnw2, knw2, bd2, ssd_consts, nseq, seq):
    m = x2d.shape[0]
    tiles_per_seq = seq // TRUNK_TILE
    ntiles = m // TRUNK_TILE
    produced = lambda j: jnp.minimum(j, ntiles - 1)
    consumed = lambda j: jnp.maximum(j - 1, 0)
    const = lambda j: (0, 0)
    tab_map = lambda j: ((j + tiles_per_seq - 1) % tiles_per_seq, 0)
    seq_map = lambda j: (consumed(j) // tiles_per_seq, 0, 0)
    in_specs = [
        pl.BlockSpec(memory_space=pltpu.SMEM),
        pl.BlockSpec((TRUNK_TILE, D_MODEL), lambda j: (produced(j), 0)),
        pl.BlockSpec((1, D_MODEL), const),
        pl.BlockSpec((D_MODEL, N_MAIN), const, pipeline_mode=pl.Buffered(1)),
        pl.BlockSpec((D_MODEL, LANES), const, pipeline_mode=pl.Buffered(1)),
        pl.BlockSpec((TRUNK_TILE, LANES), tab_map),
        pl.BlockSpec((TRUNK_TILE, LANES), tab_map),
        pl.BlockSpec((TRUNK_TILE, LANES), tab_map),
        pl.BlockSpec((1, LANES), const),
        pl.BlockSpec((1, LANES), const),
        pl.BlockSpec((2 * LANES, LANES), const),
        *_ssd_const_specs(const),
    ]
    return pl.pallas_call(
        functools.partial(_trunk_body, tiles_per_seq=tiles_per_seq),
        grid=(ntiles + 1,),
        in_specs=in_specs,
        out_specs=[
            pl.BlockSpec((TRUNK_TILE, 2 * D_MODEL), lambda j: (produced(j), 0)),
            pl.BlockSpec((TRUNK_TILE, ATTN_WIDTH), lambda j: (consumed(j), 0)),
            pl.BlockSpec((TRUNK_TILE, SSD_WIDTH), lambda j: (consumed(j), 0)),
            pl.BlockSpec((1, WINDOW, KV_WIDTH), seq_map),
            pl.BlockSpec((1, WINDOW, KV_WIDTH), seq_map),
            pl.BlockSpec((1, CONV_WIDTH - 1, CONV_DIM), seq_map),
            pl.BlockSpec((1, SSD_WIDTH, D_STATE), seq_map),
        ],
        out_shape=[
            jax.ShapeDtypeStruct((m, 2 * D_MODEL), BF16),
            jax.ShapeDtypeStruct((m, ATTN_WIDTH), BF16),
            jax.ShapeDtypeStruct((m, SSD_WIDTH), BF16),
            jax.ShapeDtypeStruct((nseq, WINDOW, KV_WIDTH), F32),
            jax.ShapeDtypeStruct((nseq, WINDOW, KV_WIDTH), F32),
            jax.ShapeDtypeStruct((nseq, CONV_WIDTH - 1, CONV_DIM), F32),
            jax.ShapeDtypeStruct((nseq, SSD_WIDTH, D_STATE), F32),
        ],
        scratch_shapes=[
            pltpu.VMEM((TRUNK_TILE, N_MAIN), BF16),
            pltpu.VMEM((TRUNK_TILE, N_MAIN), BF16),
            pltpu.VMEM((TRUNK_TILE, LANES), F32),
            pltpu.VMEM((TRUNK_TILE, LANES), F32),
            pltpu.VMEM((TRUNK_TILE, D_MODEL), BF16),
            *_attn_scratch(1, WINDOW),
            *_ssd_scratch(SSD_CHUNK, 0),
        ],
        compiler_params=pltpu.CompilerParams(
            dimension_semantics=("arbitrary",), vmem_limit_bytes=VMEM_LIMIT_BYTES),
        name="prompt_trunk",
    )(sinks, x2d, nw, w_main, w_dt, *tables, qnw2, knw2, bd2, *ssd_consts)


def _rope_tables(pos):
    half = ROPE_DIM // 2
    inv_freq = jnp.power(ROPE_THETA, -jnp.arange(half, dtype=F32) * (2.0 / ROPE_DIM))
    ang = pos.astype(F32)[:, None] * inv_freq[None, :]
    cos, sin = jnp.cos(ang), jnp.sin(ang)
    n = pos.shape[0]
    rest = HEAD_DIM - ROPE_DIM
    z_half, z_rest = jnp.zeros((n, half), F32), jnp.zeros((n, rest), F32)
    cos_t = jnp.concatenate([cos, cos, jnp.ones((n, rest), F32)], axis=1)
    sin_lo = jnp.concatenate([-sin, z_half, z_rest], axis=1)
    sin_hi = jnp.concatenate([z_half, sin, z_rest], axis=1)
    return tuple(jnp.tile(tb, (1, LANES // HEAD_DIM)) for tb in (cos_t, sin_lo, sin_hi))


def _pad_lanes(v):
    return jnp.pad(v.reshape(1, -1).astype(F32), ((0, 0), (0, LANES - v.shape[-1])))


def kernel(x_prompt, x_sample, cache_k, cache_v, state_conv, state_ssm, norm_w, w_in, q_norm_w, k_norm_w, sinks,
           conv_w, conv_b, dt_bias, A_log, D_skip, ssd_norm_w, w_attn_proj, w_ssd_proj, w_out):
    depth = norm_w.shape[0]
    assert depth == 1, "single-layer trunk"
    bp, seq = x_prompt.shape[:2]
    bs, tdec = x_sample.shape[:2]
    assert seq % TRUNK_TILE == 0 and tdec == SUBLANES and cache_k.shape[2] == WINDOW

    w = w_in[0]
    w_main = jnp.concatenate([
        w[:, OFF_ZA:OFF_ZS], w[:, OFF_DT:], w[:, :OFF_Q], w[:, OFF_V:OFF_ZA], w[:, OFF_ZS:OFF_XBC], w[:, OFF_Q:OFF_V],
    ], axis=1).astype(BF16)
    w_dt = jnp.pad(w[:, OFF_XBC:OFF_DT], ((0, 0), (0, LANES - SSD_HEADS))).astype(BF16)
    nw = norm_w[0].reshape(1, D_MODEL)
    w_ap, w_sp, w_o = w_attn_proj[0].astype(BF16), w_ssd_proj[0].astype(BF16), w_out[0].astype(BF16)

    lane = jnp.arange(LANES)
    bd = (lane[:, None] // HEAD_DIM == lane[None, :] // HEAD_DIM).astype(F32) / HEAD_DIM
    bd2 = jnp.concatenate([bd, bd], axis=0).astype(BF16)
    head_of_lane = jnp.arange(SSD_WIDTH) // SSD_HEAD_DIM
    e1 = (lane[:, None] == head_of_lane[None, :]).astype(BF16)
    e2 = jnp.concatenate([e1, e1], axis=0)
    tri = (lane[:, None] >= lane[None, :]).astype(BF16)
    shifts = jnp.concatenate([(lane[:, None] - j == lane[None, :]) for j in range(1, CONV_WIDTH)],
                             axis=0).astype(BF16)
    qnw2 = jnp.tile(q_norm_w[0].reshape(1, HEAD_DIM), (1, LANES // HEAD_DIM))
    knw2 = jnp.tile(k_norm_w[0].reshape(1, HEAD_DIM), (1, LANES // HEAD_DIM))
    ssd_consts = (conv_w[0], conv_b[0].reshape(1, CONV_DIM), _pad_lanes(dt_bias[0]), _pad_lanes(A_log[0]),
                  jnp.repeat(D_skip[0], SSD_HEAD_DIM).reshape(1, SSD_WIDTH), ssd_norm_w[0].reshape(1, SSD_WIDTH),
                  e2, tri, shifts)
    sink_vec = sinks[0].astype(F32)

    xp2 = x_prompt.reshape(bp * seq, D_MODEL)
    tables_p = _rope_tables(jnp.arange(seq, dtype=jnp.int32))
    gates_p, oa_p, os_p, kwin_p, vwin_p, conv_p, ssm_p = _trunk(
        xp2, sink_vec, nw, w_main, w_dt, tables_p, qnw2, knw2, bd2, ssd_consts, bp, seq)
    y_p = _outproj(xp2, gates_p, 0, oa_p, os_p, w_ap, w_sp, w_o, 512).reshape(bp, seq, D_MODEL)

    xs2 = x_sample.reshape(bs * tdec, D_MODEL)
    seqs_per_step = 4
    proj_s, dt_s = _inproj(xs2, nw, w_main, w_dt, 256, F32)
    pos_s = PAST_LEN + jnp.arange(tdec, dtype=jnp.int32)
    tables_s = tuple(jnp.tile(tb, (seqs_per_step, 1)) for tb in _rope_tables(pos_s))
    ck = cache_k[0].reshape(bs, WINDOW, KV_WIDTH)
    cv = cache_v[0].reshape(bs, WINDOW, KV_WIDTH)
    oa_s, kwin_s, vwin_s = _attention_decode(proj_s, sink_vec, tables_s, qnw2, knw2, bd2, (ck, cv),
                                             bs // seqs_per_step, seqs_per_step, tdec)
    h0 = state_ssm[0].reshape(bs, SSD_WIDTH, D_STATE)
    os_s, conv_s, ssm_s = _ssd_decode(proj_s, dt_s, ssd_consts, (state_conv[0], h0), bs, tdec)
    y_s = _outproj(xs2, proj_s, COL_G // (2 * D_MODEL), oa_s, os_s, w_ap, w_sp, w_o, 256).reshape(bs, tdec, D_MODEL)

    win = lambda a, n: a.reshape(1, n, WINDOW, N_KV_HEADS, HEAD_DIM)
    ssm = lambda a, n: a.reshape(1, n, SSD_HEADS, SSD_HEAD_DIM, D_STATE)
    return (y_p, y_s, win(kwin_p, bp), win(vwin_p, bp), conv_p[None], ssm(ssm_p, bp),
            win(kwin_s, bs), win(vwin_s, bs), conv_s[None], ssm(ssm_s, bs))
```

```python
import functools
import types

import jax
import jax.numpy as jnp
from jax import lax
from jax.experimental import pallas as pl
from jax.experimental.pallas import tpu as pltpu

F32 = jnp.float32
BF16 = jnp.bfloat16

D_MODEL = 1024
N_HEADS = 16
N_KV_HEADS = 4
HEAD_DIM = 64
KV_GROUP = N_HEADS // N_KV_HEADS
ATTN_WIDTH = N_HEADS * HEAD_DIM
KV_WIDTH = N_KV_HEADS * HEAD_DIM
WINDOW = 128
ROPE_DIM = HEAD_DIM // 4
ROPE_THETA = 500000.0
PAST_LEN = 16384
SSD_WIDTH = 2 * D_MODEL
SSD_HEAD_DIM = 64
SSD_HEADS = SSD_WIDTH // SSD_HEAD_DIM
SSD_GROUPS = 4
SSD_HPG = SSD_HEADS // SSD_GROUPS
D_STATE = 128
CONV_WIDTH = 4
CONV_DIM = SSD_WIDTH + 2 * SSD_GROUPS * D_STATE
SSD_CHUNK = 128
EPS = 1e-6

OFF_Q = ATTN_WIDTH
OFF_K = OFF_Q + KV_WIDTH
OFF_V = OFF_K + KV_WIDTH
OFF_ZA = OFF_V + ATTN_WIDTH
OFF_ZS = OFF_ZA + SSD_WIDTH
OFF_XBC = OFF_ZS + CONV_DIM
OFF_DT = OFF_XBC + SSD_HEADS

LANES = 128
SUBLANES = 8
VMEM_LIMIT_BYTES = 56 * 1024 * 1024

COL_ZS = 0
COL_G = COL_ZS + SSD_WIDTH
COL_Q = COL_G + 2 * D_MODEL
COL_ZA = COL_Q + ATTN_WIDTH
COL_XBC = COL_ZA + ATTN_WIDTH
COL_KV = COL_XBC + CONV_DIM
N_MAIN = COL_KV + 2 * KV_WIDTH
PROJ_TILE_N = 512
TRUNK_TILE = 2 * SSD_CHUNK
GROUP_WIDTH = SSD_WIDTH // SSD_GROUPS
HALF = LANES // 2
NEG_BIG = -1e30
LOG2E = 1.4426950408889634


def _sigmoid(x):
    return 0.5 * jnp.tanh(0.5 * x) + 0.5


def _silu(x):
    hx = 0.5 * x
    return hx * jnp.tanh(hx) + hx


def _split2(v):
    hi = v.astype(BF16)
    lo = (v - hi.astype(F32)).astype(BF16)
    return hi, lo


def _drain(gen):
    for _ in gen:
        pass


def _interleave(main, side, every):
    live = {"main": True, "side": True}

    def step(gen, key):
        if live[key]:
            try:
                next(gen)
            except StopIteration:
                live[key] = False

    while live["main"] or live["side"]:
        for _ in range(every):
            step(main, "main")
        step(side, "side")


def _inproj_gen(x_ref, nw_ref, w_ref, wdt_ref, store_cols, store_dt, h_scr):
    x = x_ref[...]
    ms = jnp.mean(x * x, axis=-1, keepdims=True)
    h_scr[...] = (x * lax.rsqrt(ms + EPS) * nw_ref[...]).astype(BF16)
    yield
    for j in range(N_MAIN // PROJ_TILE_N):
        sl = slice(j * PROJ_TILE_N, (j + 1) * PROJ_TILE_N)
        store_cols(j, jnp.dot(h_scr[...], w_ref[:, sl], preferred_element_type=F32))
        yield
    store_dt(jnp.dot(h_scr[...], wdt_ref[...], preferred_element_type=F32))
    yield


def _inproj_body(x_ref, nw_ref, w_ref, wdt_ref, out_ref, dt_ref, h_scr):
    def store_cols(j, val):
        out_ref[:, j * PROJ_TILE_N:(j + 1) * PROJ_TILE_N] = val.astype(out_ref.dtype)

    def store_dt(val):
        dt_ref[...] = val

    _drain(_inproj_gen(x_ref, nw_ref, w_ref, wdt_ref, store_cols, store_dt, h_scr))


def _inproj(x2d, norm_w, w_main, w_dt, tm, out_dtype):
    m = x2d.shape[0]
    return pl.pallas_call(
        _inproj_body,
        grid=(m // tm,),
        in_specs=[
            pl.BlockSpec((tm, D_MODEL), lambda i: (i, 0)),
            pl.BlockSpec((1, D_MODEL), lambda i: (0, 0)),
            pl.BlockSpec((D_MODEL, N_MAIN), lambda i: (0, 0), pipeline_mode=pl.Buffered(1)),
            pl.BlockSpec((D_MODEL, LANES), lambda i: (0, 0), pipeline_mode=pl.Buffered(1)),
        ],
        out_specs=[
            pl.BlockSpec((tm, N_MAIN), lambda i: (i, 0)),
            pl.BlockSpec((tm, LANES), lambda i: (i, 0)),
        ],
        out_shape=[
            jax.ShapeDtypeStruct((m, N_MAIN), out_dtype),
            jax.ShapeDtypeStruct((m, LANES), F32),
        ],
        scratch_shapes=[pltpu.VMEM((tm, D_MODEL), BF16)],
        compiler_params=pltpu.CompilerParams(
            dimension_semantics=("arbitrary",), vmem_limit_bytes=VMEM_LIMIT_BYTES),
        name="inproj",
    )(x2d, norm_w, w_main, w_dt)


def _outproj_body(x_ref, g_ref, oa_ref, os_ref, wa_ref, ws_ref, wo_ref, y_ref):
    pa = jnp.dot(oa_ref[...].astype(BF16), wa_ref[...], preferred_element_type=F32)
    ps = jnp.dot(os_ref[...].astype(BF16), ws_ref[...], preferred_element_type=F32)
    g = g_ref[...].astype(F32)
    merged = _sigmoid(g[:, :D_MODEL]) * pa + _sigmoid(g[:, D_MODEL:]) * ps
    y_ref[...] = x_ref[...] + jnp.dot(merged.astype(BF16), wo_ref[...], preferred_element_type=F32)


def _outproj(x2d, gates, gate_col_block, o_a, o_s, w_ap, w_sp, w_o, tm):
    m = x2d.shape[0]
    const = lambda i: (0, 0)
    return pl.pallas_call(
        _outproj_body,
        grid=(m // tm,),
        in_specs=[
            pl.BlockSpec((tm, D_MODEL), lambda i: (i, 0)),
            pl.BlockSpec((tm, 2 * D_MODEL), lambda i: (i, gate_col_block)),
            pl.BlockSpec((tm, ATTN_WIDTH), lambda i: (i, 0)),
            pl.BlockSpec((tm, SSD_WIDTH), lambda i: (i, 0)),
            pl.BlockSpec((ATTN_WIDTH, D_MODEL), const, pipeline_mode=pl.Buffered(1)),
            pl.BlockSpec((SSD_WIDTH, D_MODEL), const, pipeline_mode=pl.Buffered(1)),
            pl.BlockSpec((D_MODEL, D_MODEL), const, pipeline_mode=pl.Buffered(1)),
        ],
        out_specs=pl.BlockSpec((tm, D_MODEL), lambda i: (i, 0)),
        out_shape=jax.ShapeDtypeStruct((m, D_MODEL), F32),
        compiler_params=pltpu.CompilerParams(
            dimension_semantics=("arbitrary",), vmem_limit_bytes=VMEM_LIMIT_BYTES),
        name="outproj",
    )(x2d, gates, o_a, o_s, w_ap, w_sp, w_o)


def _norm_rope(xc, nw, bd2, cos_t, sin_lo, sin_hi):
    hi, lo = _split2(xc * xc)
    ms = jnp.dot(jnp.concatenate([hi, lo], axis=1), bd2, preferred_element_type=F32)
    xn = xc * lax.rsqrt(ms + EPS) * nw
    return xn * cos_t + pltpu.roll(xn, LANES - ROPE_DIM // 2, 1) * sin_lo + pltpu.roll(xn, ROPE_DIM // 2, 1) * sin_hi


def _dup_half(chunk, keep_low, lane_lo):
    rolled = pltpu.roll(chunk, HALF, 1)
    return jnp.where(lane_lo, chunk, rolled) if keep_low else jnp.where(lane_lo, rolled, chunk)


def _attn_reset(kdup, vdup):
    for g in range(N_KV_HEADS):
        kdup[0, g, WINDOW:2 * WINDOW, :] = jnp.zeros((WINDOW, LANES), BF16)
        vdup[0, g, WINDOW:2 * WINDOW, 0:LANES] = jnp.zeros((WINDOW, LANES), BF16)
    vdup[:, :, :, LANES:2 * LANES] = jnp.ones((vdup.shape[0], N_KV_HEADS, 2 * WINDOW, LANES), BF16)


def _attn_gen(io, sinks_ref, qnw_ref, knw_ref, bd2_ref, kdup, vdup, o_scr, *, seqs, t, decode, prev_valid,
              own_half):
    tp = max(t, 2 * SUBLANES)
    nkeys = 2 * WINDOW
    own_lo, prev_lo = own_half * WINDOW, (1 - own_half) * WINDOW
    bd2 = bd2_ref[...]
    cos_t, sin_lo, sin_hi = io.tables()
    lane_lo = lax.broadcasted_iota(jnp.int32, (1, LANES), 1) < HALF

    kv = io.kv().astype(F32)
    k_rot = [_norm_rope(kv[:, c * LANES:(c + 1) * LANES], knw_ref[...], bd2, cos_t, sin_lo, sin_hi)
             for c in range(KV_WIDTH // LANES)]
    v_raw = [kv[:, KV_WIDTH + c * LANES:KV_WIDTH + (c + 1) * LANES] for c in range(KV_WIDTH // LANES)]
    yield
    q_all = io.q().astype(F32)
    q_rot = []
    for c in range(ATTN_WIDTH // LANES):
        q_rot.append(_norm_rope(q_all[:, c * LANES:(c + 1) * LANES], qnw_ref[...], bd2, cos_t, sin_lo, sin_hi)
                     * (HEAD_DIM ** -0.5))
        if c % 2 == 1:
            yield

    if decode:
        vdup[:, :, :, LANES:2 * LANES] = jnp.ones((seqs, N_KV_HEADS, nkeys, LANES), BF16)

    own = (lax.broadcasted_iota(jnp.int32, (t, WINDOW), 1) <= lax.broadcasted_iota(jnp.int32, (t, WINDOW), 0))
    prev_bias = None if prev_valid is None else jnp.where(prev_valid, 0.0, NEG_BIG)

    def pad_rows(a):
        if tp == t:
            return a
        return jnp.concatenate([a, jnp.zeros((tp - t, a.shape[1]), a.dtype)], axis=0)

    for si in range(seqs):
        rs = slice(si * t, (si + 1) * t)
        for g in range(N_KV_HEADS):
            c, low = g // 2, (g % 2 == 0)
            kdup[si, g, own_lo:own_lo + tp, :] = pad_rows(_dup_half(k_rot[c][rs], low, lane_lo)).astype(BF16)
            vdup[si, g, own_lo:own_lo + tp, 0:LANES] = pad_rows(_dup_half(v_raw[c][rs], low, lane_lo)).astype(BF16)
            if decode:
                kdup[si, g, prev_lo:prev_lo + WINDOW, :] = _dup_half(
                    io.ck(si)[:, c * LANES:(c + 1) * LANES], low, lane_lo).astype(BF16)
                vdup[si, g, prev_lo:prev_lo + WINDOW, 0:LANES] = _dup_half(
                    io.cv(si)[:, c * LANES:(c + 1) * LANES], low, lane_lo).astype(BF16)
                kdup[si, g, own_lo + tp:own_lo + WINDOW, :] = jnp.zeros((WINDOW - tp, LANES), BF16)
                vdup[si, g, own_lo + tp:own_lo + WINDOW, 0:LANES] = jnp.zeros((WINDOW - tp, LANES), BF16)
        yield
        scores = []
        for g in range(N_KV_HEADS):
            q_stack = jnp.concatenate(
                [jnp.where(lane_lo if j % 2 == 0 else jnp.logical_not(lane_lo), q_rot[j // 2][rs], 0.0)
                 for j in range(KV_GROUP * g, KV_GROUP * (g + 1))], axis=0).astype(BF16)
            scores.append(lax.dot_general(q_stack, kdup[si, g], (((1,), (1,)), ((), ())),
                                          preferred_element_type=F32))
            yield
        probs, sink_terms = [], []
        for j in range(N_HEADS):
            s = scores[j // KV_GROUP]
            r0 = (j % KV_GROUP) * t
            s_prev = s[r0:r0 + t, prev_lo:prev_lo + WINDOW]
            if prev_bias is not None:
                s_prev = s_prev + prev_bias
            sj = jnp.where(own, s[r0:r0 + t, own_lo:own_lo + WINDOW], s_prev)
            sink = sinks_ref[j]
            mx = jnp.maximum(jnp.max(sj, axis=-1, keepdims=True), sink)
            p = jnp.exp(sj - mx)
            sink_terms.append(jnp.exp(sink - mx))
            halves = [jnp.where(own, 0.0, p), jnp.where(own, p, 0.0)]
            probs.append(jnp.concatenate(halves if own_half == 1 else halves[::-1], axis=1))
            if j % KV_GROUP == KV_GROUP - 1:
                yield
        for g in range(N_KV_HEADS):
            p_all = jnp.concatenate(probs[KV_GROUP * g:KV_GROUP * (g + 1)], axis=0).astype(BF16)
            o_aug = jnp.dot(p_all, vdup[si, g], preferred_element_type=F32)
            o = [o_aug[jj * t:(jj + 1) * t, 0:LANES]
                 / (o_aug[jj * t:(jj + 1) * t, LANES:2 * LANES] + sink_terms[KV_GROUP * g + jj])
                 for jj in range(KV_GROUP)]
            for half in range(2):
                o_scr[rs, (2 * g + half) * LANES:(2 * g + half + 1) * LANES] = jnp.where(
                    lane_lo, o[2 * half], o[2 * half + 1])
            yield

    io.oa(o_scr[...] * _silu(io.za().astype(F32)))
    yield

    k_new = jnp.concatenate(k_rot, axis=1)
    v_new = jnp.concatenate(v_raw, axis=1)
    if decode:
        for si in range(seqs):
            io.kwin(si, 0, WINDOW - t, io.ck(si)[t:WINDOW, :])
            io.vwin(si, 0, WINDOW - t, io.cv(si)[t:WINDOW, :])
            io.kwin(si, WINDOW - t, WINDOW, k_new[si * t:(si + 1) * t])
            io.vwin(si, WINDOW - t, WINDOW, v_new[si * t:(si + 1) * t])
    else:
        io.kwin(0, 0, WINDOW, k_new)
        io.vwin(0, 0, WINDOW, v_new)
    yield


def _attn_scratch(seqs, rows):
    return [
        pltpu.VMEM((seqs, N_KV_HEADS, 2 * WINDOW, LANES), BF16),
        pltpu.VMEM((seqs, N_KV_HEADS, 2 * WINDOW, 2 * LANES), BF16),
        pltpu.VMEM((rows, ATTN_WIDTH), F32),
    ]


def _attn_decode_body(sinks_ref, q_ref, kv_ref, za_ref, cos_ref, slo_ref, shi_ref, qnw_ref, knw_ref, bd2_ref,
                      ck_ref, cv_ref, oa_ref, kwin_ref, vwin_ref, kdup, vdup, o_scr, *, seqs, t):
    def store_oa(val):
        oa_ref[...] = val.astype(oa_ref.dtype)

    def store_kwin(si, lo, hi, val):
        kwin_ref[si, lo:hi, :] = val

    def store_vwin(si, lo, hi, val):
        vwin_ref[si, lo:hi, :] = val

    io = types.SimpleNamespace(
        q=lambda: q_ref[...], kv=lambda: kv_ref[...], za=lambda: za_ref[...],
        tables=lambda: (cos_ref[...], slo_ref[...], shi_ref[...]),
        oa=store_oa, kwin=store_kwin, vwin=store_vwin, ck=lambda si: ck_ref[si], cv=lambda si: cv_ref[si])
    _drain(_attn_gen(io, sinks_ref, qnw_ref, knw_ref, bd2_ref, kdup, vdup, o_scr,
                     seqs=seqs, t=t, decode=True, prev_valid=None, own_half=1))


def _attention_decode(proj, sinks, tables, qnw2, knw2, bd2, caches, nsteps, seqs, t):
    rows = seqs * t
    m = proj.shape[0]
    const = lambda s: (0, 0)
    in_specs = [
        pl.BlockSpec(memory_space=pltpu.SMEM),
        pl.BlockSpec((rows, ATTN_WIDTH), lambda s: (s, COL_Q // ATTN_WIDTH)),
        pl.BlockSpec((rows, 2 * KV_WIDTH), lambda s: (s, COL_KV // (2 * KV_WIDTH))),
        pl.BlockSpec((rows, ATTN_WIDTH), lambda s: (s, COL_ZA // ATTN_WIDTH)),
        pl.BlockSpec((rows, LANES), const),
        pl.BlockSpec((rows, LANES), const),
        pl.BlockSpec((rows, LANES), const),
        pl.BlockSpec((1, LANES), const),
        pl.BlockSpec((1, LANES), const),
        pl.BlockSpec((2 * LANES, LANES), const),
        pl.BlockSpec((seqs, WINDOW, KV_WIDTH), lambda s: (s, 0, 0)),
        pl.BlockSpec((seqs, WINDOW, KV_WIDTH), lambda s: (s, 0, 0)),
    ]
    return pl.pallas_call(
        functools.partial(_attn_decode_body, seqs=seqs, t=t),
        grid=(nsteps,),
        in_specs=in_specs,
        out_specs=[
            pl.BlockSpec((rows, ATTN_WIDTH), lambda s: (s, 0)),
            pl.BlockSpec((seqs, WINDOW, KV_WIDTH), lambda s: (s, 0, 0)),
            pl.BlockSpec((seqs, WINDOW, KV_WIDTH), lambda s: (s, 0, 0)),
        ],
        out_shape=[
            jax.ShapeDtypeStruct((m, ATTN_WIDTH), F32),
            jax.ShapeDtypeStruct((nsteps * seqs, WINDOW, KV_WIDTH), F32),
            jax.ShapeDtypeStruct((nsteps * seqs, WINDOW, KV_WIDTH), F32),
        ],
        scratch_shapes=_attn_scratch(seqs, rows),
        compiler_params=pltpu.CompilerParams(
            dimension_semantics=("arbitrary",), vmem_limit_bytes=VMEM_LIMIT_BYTES),
        name="attn_decode",
    )(sinks, proj, proj, proj, *tables, qnw2, knw2, bd2, *caches)


def _ssd_gen(io, cw_ref, cb_ref, dtb_ref, alog_ref, dsk_ref, nw_ref, e2_ref, tri_ref,
             xp, xs_scr, bc_scr, acs_scr, acst_scr, dtt_scr, ht, y_scr, *, q, decode):
    lc = SSD_CHUNK
    tail = CONV_WIDTH - 1
    base = SUBLANES

    slab = 4 * LANES
    row8 = lax.broadcasted_iota(jnp.int32, (SUBLANES, 1), 0)
    for cs in range(CONV_DIM // slab):
        cl = slice(cs * slab, (cs + 1) * slab)
        if decode:
            acc = cb_ref[:, cl] + cw_ref[0:1, cl] * xp[base - tail:base - tail + q, cl]
            for j in range(1, CONV_WIDTH):
                acc = acc + cw_ref[j:j + 1, cl] * xp[base - tail + j:base - tail + j + q, cl]
        else:
            xe = jnp.concatenate([xp[:, cl], io.xbc(cl).astype(F32)], axis=0)
            acc = cw_ref[0:1, cl] * xe
            for j in range(1, CONV_WIDTH):
                acc = pltpu.roll(acc, 1, 0) + cw_ref[j:j + 1, cl] * xe
            acc = acc[SUBLANES:] + cb_ref[:, cl]
        act = _silu(acc)
        if cs < SSD_WIDTH // slab:
            xs_scr[0:q, cl] = act
        else:
            bc_scr[0:q, cs * slab - SSD_WIDTH:(cs + 1) * slab - SSD_WIDTH] = act
        yield
    if not decode:
        xp[...] = io.xbc_last_rows().astype(F32)[SUBLANES:2 * SUBLANES]

    dt_in = io.dt()
    if q < lc:
        dt_in = jnp.concatenate([dt_in, jnp.zeros((lc - q, LANES), F32)], axis=0)
    xdt = dt_in + dtb_ref[...]
    dt = jnp.maximum(xdt, 0.0) + jnp.log1p(jnp.exp(-jnp.abs(xdt)))
    if q < lc:
        dt = jnp.where(lax.broadcasted_iota(jnp.int32, (lc, LANES), 0) < q, dt, 0.0)
    a = dt * (-jnp.exp(alog_ref[...]))
    a_hi = a.astype(BF16)
    a_r1 = a - a_hi.astype(F32)
    a_mid = a_r1.astype(BF16)
    a_lo = (a_r1 - a_mid.astype(F32)).astype(BF16)
    acs3 = jnp.dot(tri_ref[...], jnp.concatenate([a_hi, a_mid, a_lo], axis=1), preferred_element_type=F32)
    acs = acs3[:, 0:LANES] + acs3[:, LANES:2 * LANES] + acs3[:, 2 * LANES:3 * LANES]
    acs2 = acs * LOG2E
    acs_scr[...] = acs2
    acst_scr[...] = acs2.T
    dtt_scr[...] = dt.T
    yield

    def expand(v):
        hi, lo = _split2(v)
        return jnp.dot(jnp.concatenate([hi, lo], axis=1), e2_ref[...], preferred_element_type=F32)

    both = expand(jnp.concatenate([jnp.exp(acs[0:q]), (jnp.exp(acs[lc - 1:lc, :] - acs) * dt)[0:q]], axis=0))
    e_off, w_state = both[0:q], both[q:2 * q]
    chunk_decay = e_off[q - 1:q, :]
    yield

    causal = (lax.broadcasted_iota(jnp.int32, (q, lc), 0) >= lax.broadcasted_iota(jnp.int32, (q, lc), 1))
    lane_lo = lax.broadcasted_iota(jnp.int32, (1, LANES), 1) < HALF
    nbc = SSD_GROUPS * D_STATE
    pairs_per_group = SSD_HPG // 2
    gls = [slice(g * GROUP_WIDTH, (g + 1) * GROUP_WIDTH) for g in range(SSD_GROUPS)]
    b_gs = [bc_scr[:, g * D_STATE:(g + 1) * D_STATE] for g in range(SSD_GROUPS)]
    c_gs = [bc_scr[0:q, nbc + g * D_STATE:nbc + (g + 1) * D_STATE].astype(BF16) for g in range(SSD_GROUPS)]
    cbs = [lax.dot_general(c_gs[g], b_gs[g].astype(BF16), (((1,), (1,)), ((), ())), preferred_element_type=F32)
           for g in range(SSD_GROUPS)]
    yield
    y_offs = []
    for g in range(SSD_GROUPS):
        y_offs.append(jnp.dot(c_gs[g], ht[g].astype(BF16), preferred_element_type=F32) * e_off[0:q, gls[g]])
        yield
    for g in range(SSD_GROUPS):
        xw = xs_scr[0:q, gls[g]] * w_state[:, gls[g]]
        if q < lc:
            xw = jnp.concatenate([xw, jnp.zeros((lc - q, GROUP_WIDTH), F32)], axis=0)
        xw = xw.astype(BF16)
        ht[g] = ht[g] * chunk_decay[:, gls[g]] + jnp.dot(b_gs[g].T.astype(BF16), xw, preferred_element_type=F32)
        yield
    w_pairs = []
    for pair in range(SSD_HEADS // 2):
        w_parts = []
        for h in (2 * pair, 2 * pair + 1):
            seg2 = acs_scr[0:q, h:h + 1] - acst_scr[h:h + 1, :]
            decay = jnp.exp2(jnp.where(causal, seg2, -jnp.inf))
            w_parts.append(cbs[pair // pairs_per_group] * decay * dtt_scr[h:h + 1, :])
        w_pairs.append(jnp.concatenate(w_parts, axis=1).astype(BF16))
        if pair % 2 == 1:
            yield
    y_diags = []
    for pair in range(SSD_HEADS // 2):
        x_pair = xs_scr[:, pair * LANES:(pair + 1) * LANES]
        x_bd = jnp.concatenate([jnp.where(lane_lo, x_pair, 0.0), jnp.where(lane_lo, 0.0, x_pair)],
                               axis=0).astype(BF16)
        y_diags.append(jnp.dot(w_pairs[pair], x_bd, preferred_element_type=F32))
        if pair % 4 == 3:
            yield
    for pair in range(SSD_HEADS // 2):
        g, pr = divmod(pair, pairs_per_group)
        pl_ = slice(pair * LANES, (pair + 1) * LANES)
        y_pair = y_diags[pair] + y_offs[g][:, pr * LANES:(pr + 1) * LANES] + dsk_ref[:, pl_] * xs_scr[0:q, pl_]
        y_scr[:, pl_] = y_pair * _silu(io.zs(pl_).astype(F32))
        if pair % 4 == 3:
            yield
    for g in range(SSD_GROUPS):
        yg = y_scr[:, gls[g]]
        ms = jnp.mean(yg * yg, axis=-1, keepdims=True)
        io.os(gls[g], yg * lax.rsqrt(ms + EPS) * nw_ref[:, gls[g]])
        yield


def _ssd_write_state(hout_ref, ht):
    for g in range(SSD_GROUPS):
        hout_ref[0, g * GROUP_WIDTH:(g + 1) * GROUP_WIDTH, :] = ht[g].T


def _ssd_scratch(q, staged_rows):
    return [
        pltpu.VMEM((SUBLANES + staged_rows, CONV_DIM), F32),
        pltpu.VMEM((SSD_CHUNK, SSD_WIDTH), F32),
        pltpu.VMEM((SSD_CHUNK, 2 * SSD_GROUPS * D_STATE), F32),
        pltpu.VMEM((SSD_CHUNK, LANES), F32),
        pltpu.VMEM((LANES, SSD_CHUNK), F32),
        pltpu.VMEM((LANES, SSD_CHUNK), F32),
        pltpu.VMEM((SSD_GROUPS, D_STATE, GROUP_WIDTH), F32),
        pltpu.VMEM((q, SSD_WIDTH), F32),
    ]


def _ssd_decode_body(xbc_ref, zs_ref, dt_ref, cw_ref, cb_ref, dtb_ref, alog_ref, dsk_ref, nw_ref, e2_ref, tri_ref,
                     cst_ref, h0_ref, os_ref, cout_ref, hout_ref, *scratch, q, seqs):
    tail = CONV_WIDTH - 1
    base = SUBLANES
    per_seq = len(scratch) // seqs
    gens = []
    for si in range(seqs):
        xp, xs_scr, bc_scr, acs_scr, acst_scr, dtt_scr, ht, y_scr = scratch[si * per_seq:(si + 1) * per_seq]
        rows = slice(si * q, (si + 1) * q)

        @pl.when(pl.program_id(0) == 0)
        def _(xs_scr=xs_scr, bc_scr=bc_scr):
            xs_scr[q:SSD_CHUNK, :] = jnp.zeros((SSD_CHUNK - q, SSD_WIDTH), F32)
            bc_scr[q:SSD_CHUNK, :] = jnp.zeros((SSD_CHUNK - q, 2 * SSD_GROUPS * D_STATE), F32)

        xp[base - tail:base, :] = cst_ref[si]
        xp[base:base + q, :] = xbc_ref[rows, :].astype(F32)
        for g in range(SSD_GROUPS):
            ht[g] = h0_ref[si, g * GROUP_WIDTH:(g + 1) * GROUP_WIDTH, :].T

        def store_os(cols, val, rows=rows):
            os_ref[rows, cols] = val.astype(os_ref.dtype)

        io = types.SimpleNamespace(zs=lambda cols, rows=rows: zs_ref[rows, cols], dt=lambda rows=rows: dt_ref[rows, :],
                                   os=store_os)
        gens.append(_ssd_gen(io, cw_ref, cb_ref, dtb_ref, alog_ref, dsk_ref, nw_ref, e2_ref, tri_ref,
                             xp, xs_scr, bc_scr, acs_scr, acst_scr, dtt_scr, ht, y_scr, q=q, decode=True))
    while gens:
        for gen in list(gens):
            try:
                next(gen)
            except StopIteration:
                gens.remove(gen)
    for si in range(seqs):
        xp, ht = scratch[si * per_seq], scratch[si * per_seq + 6]
        cout_ref[si] = xp[base + q - tail:base + q, :]
        for g in range(SSD_GROUPS):
            hout_ref[si, g * GROUP_WIDTH:(g + 1) * GROUP_WIDTH, :] = ht[g].T


def _ssd_const_specs(const):
    return [
        pl.BlockSpec((CONV_WIDTH, CONV_DIM), const),
        pl.BlockSpec((1, CONV_DIM), const),
        pl.BlockSpec((1, LANES), const),
        pl.BlockSpec((1, LANES), const),
        pl.BlockSpec((1, SSD_WIDTH), const),
        pl.BlockSpec((1, SSD_WIDTH), const),
        pl.BlockSpec((2 * LANES, SSD_WIDTH), const),
        pl.BlockSpec((SSD_CHUNK, SSD_CHUNK), const),
    ]


def _ssd_decode(proj, dt_raw, consts, states, nseq, q, seqs):
    m = proj.shape[0]
    rows = seqs * q
    const = lambda s: (0, 0)
    in_specs = [
        pl.BlockSpec((rows, CONV_DIM), lambda s: (s, COL_XBC // CONV_DIM)),
        pl.BlockSpec((rows, SSD_WIDTH), lambda s: (s, COL_ZS // SSD_WIDTH)),
        pl.BlockSpec((rows, LANES), lambda s: (s, 0)),
        *_ssd_const_specs(const),
        pl.BlockSpec((seqs, CONV_WIDTH - 1, CONV_DIM), lambda s: (s, 0, 0)),
        pl.BlockSpec((seqs, SSD_WIDTH, D_STATE), lambda s: (s, 0, 0)),
    ]
    return pl.pallas_call(
        functools.partial(_ssd_decode_body, q=q, seqs=seqs),
        grid=(nseq // seqs,),
        in_specs=in_specs,
        out_specs=[
            pl.BlockSpec((rows, SSD_WIDTH), lambda s: (s, 0)),
            pl.BlockSpec((seqs, CONV_WIDTH - 1, CONV_DIM), lambda s: (s, 0, 0)),
            pl.BlockSpec((seqs, SSD_WIDTH, D_STATE), lambda s: (s, 0, 0)),
        ],
        out_shape=[
            jax.ShapeDtypeStruct((m, SSD_WIDTH), F32),
            jax.ShapeDtypeStruct((nseq, CONV_WIDTH - 1, CONV_DIM), F32),
            jax.ShapeDtypeStruct((nseq, SSD_WIDTH, D_STATE), F32),
        ],
        scratch_shapes=_ssd_scratch(q, q) * seqs,
        compiler_params=pltpu.CompilerParams(
            dimension_semantics=("arbitrary",), vmem_limit_bytes=VMEM_LIMIT_BYTES),
        name="ssd_decode",
    )(proj, proj, dt_raw, *consts, *states)


def _trunk_body(sinks_ref, x_ref, nw_ref, w_ref, wdt_ref, cos_ref, slo_ref, shi_ref, qnw_ref, knw_ref, bd2_ref,
                cw_ref, cb_ref, dtb_ref, alog_ref, dsk_ref, snw_ref, e2_ref, tri_ref,
                g_ref, oa_ref, os_ref, kwin_ref, vwin_ref, cout_ref, hout_ref,
                proj_a, proj_b, dt_a, dt_b, h_scr, kdup, vdup, o_scr,
                xp, xs_scr, bc_scr, acs_scr, acst_scr, dtt_scr, ht, y_scr, *, tiles_per_seq):
    j = pl.program_id(0)
    tile_in_seq = (j + tiles_per_seq - 1) % tiles_per_seq
    seq_start = tile_in_seq == 0
    chunks = TRUNK_TILE // SSD_CHUNK
    assert chunks % 2 == 0, "chunk parity picks the key/value scratch half across tiles"
    g_tiles = range(COL_G // PROJ_TILE_N, COL_Q // PROJ_TILE_N)

    @pl.when(j == 0)
    def _():
        proj_b[...] = jnp.zeros(proj_b.shape, proj_b.dtype)
        dt_b[...] = jnp.zeros(dt_b.shape, F32)

    @pl.when(seq_start | (j == 0))
    def _():
        _attn_reset(kdup, vdup)
        xp[...] = jnp.zeros(xp.shape, F32)
        ht[...] = jnp.zeros(ht.shape, F32)

    def step(wbuf, wdt, rbuf, rdt):
        def store_cols(jn, val):
            if jn in g_tiles:
                g_ref[:, (jn - g_tiles[0]) * PROJ_TILE_N:(jn - g_tiles[0] + 1) * PROJ_TILE_N] = val.astype(g_ref.dtype)
            else:
                wbuf[:, jn * PROJ_TILE_N:(jn + 1) * PROJ_TILE_N] = val.astype(wbuf.dtype)

        def store_dt(val):
            wdt[...] = val

        producer = _inproj_gen(x_ref, nw_ref, w_ref, wdt_ref, store_cols, store_dt, h_scr)

        def consumer():
            for c in range(chunks):
                rows = slice(c * SSD_CHUNK, (c + 1) * SSD_CHUNK)

                def store_oa(val, rows=rows):
                    oa_ref[rows, :] = val.astype(oa_ref.dtype)

                def store_kwin(si, lo, hi, val):
                    kwin_ref[0, lo:hi, :] = val

                def store_vwin(si, lo, hi, val):
                    vwin_ref[0, lo:hi, :] = val

                def store_os(cols, val, rows=rows):
                    os_ref[rows, cols] = val.astype(os_ref.dtype)

                attn_io = types.SimpleNamespace(
                    q=lambda rows=rows: rbuf[rows, COL_Q:COL_Q + ATTN_WIDTH],
                    kv=lambda rows=rows: rbuf[rows, COL_KV:COL_KV + 2 * KV_WIDTH],
                    za=lambda rows=rows: rbuf[rows, COL_ZA:COL_ZA + ATTN_WIDTH],
                    tables=lambda rows=rows: (cos_ref[rows, :], slo_ref[rows, :], shi_ref[rows, :]),
                    oa=store_oa, kwin=store_kwin, vwin=store_vwin)
                ssd_io = types.SimpleNamespace(
                    xbc=lambda cols, rows=rows: rbuf[rows, COL_XBC + cols.start:COL_XBC + cols.stop],
                    xbc_last_rows=lambda c=c: rbuf[(c + 1) * SSD_CHUNK - 2 * SUBLANES:(c + 1) * SSD_CHUNK,
                                                   COL_XBC:COL_XBC + CONV_DIM],
                    zs=lambda cols, rows=rows: rbuf[rows, COL_ZS + cols.start:COL_ZS + cols.stop],
                    dt=lambda rows=rows: rdt[rows, :], os=store_os)
                prev_valid = (tile_in_seq > 0) if c == 0 else True
                attn = _attn_gen(attn_io, sinks_ref, qnw_ref, knw_ref, bd2_ref, kdup, vdup, o_scr,
                                 seqs=1, t=WINDOW, decode=False, prev_valid=prev_valid, own_half=c % 2)
                ssd = _ssd_gen(ssd_io, cw_ref, cb_ref, dtb_ref, alog_ref, dsk_ref, snw_ref, e2_ref, tri_ref,
                               xp, xs_scr, bc_scr, acs_scr, acst_scr, dtt_scr, ht, y_scr, q=SSD_CHUNK, decode=False)
                live = [attn, ssd]
                while live:
                    for gen in list(live):
                        try:
                            next(gen)
                        except StopIteration:
                            live.remove(gen)
                        yield

        _interleave(consumer(), producer, every=4)

    pl.when(j % 2 == 0)(lambda: step(proj_a, dt_a, proj_b, dt_b))
    pl.when(j % 2 == 1)(lambda: step(proj_b, dt_b, proj_a, dt_a))

    @pl.when((tile_in_seq == tiles_per_seq - 1) & (j > 0))
    def _():
        cout_ref[0] = xp[SUBLANES - (CONV_WIDTH - 1):SUBLANES, :]
        _ssd_write_state(hout_ref, ht)


def _trunk(x2d, sinks, nw, w_main, w_dt, tables, qnw2, knw2, bd2, ssd_consts, nseq, seq):
    m = x2d.shape[0]
    tiles_per_seq = seq // TRUNK_TILE
    ntiles = m // TRUNK_TILE
    produced = lambda j: jnp.minimum(j, ntiles - 1)
    consumed = lambda j: jnp.maximum(j - 1, 0)
    const = lambda j: (0, 0)
    tab_map = lambda j: ((j + tiles_per_seq - 1) % tiles_per_seq, 0)
    seq_map = lambda j: (consumed(j) // tiles_per_seq, 0, 0)
    in_specs = [
        pl.BlockSpec(memory_space=pltpu.SMEM),
        pl.BlockSpec((TRUNK_TILE, D_MODEL), lambda j: (produced(j), 0)),
        pl.BlockSpec((1, D_MODEL), const),
        pl.BlockSpec((D_MODEL, N_MAIN), const, pipeline_mode=pl.Buffered(1)),
        pl.BlockSpec((D_MODEL, LANES), const, pipeline_mode=pl.Buffered(1)),
        pl.BlockSpec((TRUNK_TILE, LANES), tab_map),
        pl.BlockSpec((TRUNK_TILE, LANES), tab_map),
        pl.BlockSpec((TRUNK_TILE, LANES), tab_map),
        pl.BlockSpec((1, LANES), const),
        pl.BlockSpec((1, LANES), const),
        pl.BlockSpec((2 * LANES, LANES), const),
        *_ssd_const_specs(const),
    ]
    return pl.pallas_call(
        functools.partial(_trunk_body, tiles_per_seq=tiles_per_seq),
        grid=(ntiles + 1,),
        in_specs=in_specs,
        out_specs=[
            pl.BlockSpec((TRUNK_TILE, 2 * D_MODEL), lambda j: (produced(j), 0)),
            pl.BlockSpec((TRUNK_TILE, ATTN_WIDTH), lambda j: (consumed(j), 0)),
            pl.BlockSpec((TRUNK_TILE, SSD_WIDTH), lambda j: (consumed(j), 0)),
            pl.BlockSpec((1, WINDOW, KV_WIDTH), seq_map),
            pl.BlockSpec((1, WINDOW, KV_WIDTH), seq_map),
            pl.BlockSpec((1, CONV_WIDTH - 1, CONV_DIM), seq_map),
            pl.BlockSpec((1, SSD_WIDTH, D_STATE), seq_map),
        ],
        out_shape=[
            jax.ShapeDtypeStruct((m, 2 * D_MODEL), BF16),
            jax.ShapeDtypeStruct((m, ATTN_WIDTH), BF16),
            jax.ShapeDtypeStruct((m, SSD_WIDTH), BF16),
            jax.ShapeDtypeStruct((nseq, WINDOW, KV_WIDTH), F32),
            jax.ShapeDtypeStruct((nseq, WINDOW, KV_WIDTH), F32),
            jax.ShapeDtypeStruct((nseq, CONV_WIDTH - 1, CONV_DIM), F32),
            jax.ShapeDtypeStruct((nseq, SSD_WIDTH, D_STATE), F32),
        ],
        scratch_shapes=[
            pltpu.VMEM((TRUNK_TILE, N_MAIN), BF16),
            pltpu.VMEM((TRUNK_TILE, N_MAIN), BF16),
            pltpu.VMEM((TRUNK_TILE, LANES), F32),
            pltpu.VMEM((TRUNK_TILE, LANES), F32),
            pltpu.VMEM((TRUNK_TILE, D_MODEL), BF16),
            *_attn_scratch(1, WINDOW),
            *_ssd_scratch(SSD_CHUNK, 0),
        ],
        compiler_params=pltpu.CompilerParams(
            dimension_semantics=("arbitrary",), vmem_limit_bytes=VMEM_LIMIT_BYTES),
        name="prompt_trunk",
    )(sinks, x2d, nw, w_main, w_dt, *tables, qnw2, knw2, bd2, *ssd_consts)


def _rope_tables(pos):
    half = ROPE_DIM // 2
    inv_freq = jnp.power(ROPE_THETA, -jnp.arange(half, dtype=F32) * (2.0 / ROPE_DIM))
    ang = pos.astype(F32)[:, None] * inv_freq[None, :]
    cos, sin = jnp.cos(ang), jnp.sin(ang)
    n = pos.shape[0]
    rest = HEAD_DIM - ROPE_DIM
    z_half, z_rest = jnp.zeros((n, half), F32), jnp.zeros((n, rest), F32)
    cos_t = jnp.concatenate([cos, cos, jnp.ones((n, rest), F32)], axis=1)
    sin_lo = jnp.concatenate([-sin, z_half, z_rest], axis=1)
    sin_hi = jnp.concatenate([z_half, sin, z_rest], axis=1)
    return tuple(jnp.tile(tb, (1, LANES // HEAD_DIM)) for tb in (cos_t, sin_lo, sin_hi))


def _pad_lanes(v):
    return jnp.pad(v.reshape(1, -1).astype(F32), ((0, 0), (0, LANES - v.shape[-1])))


def kernel(x_prompt, x_sample, cache_k, cache_v, state_conv, state_ssm, norm_w, w_in, q_norm_w, k_norm_w, sinks,
           conv_w, conv_b, dt_bias, A_log, D_skip, ssd_norm_w, w_attn_proj, w_ssd_proj, w_out):
    depth = norm_w.shape[0]
    assert depth == 1, "single-layer trunk"
    bp, seq = x_prompt.shape[:2]
    bs, tdec = x_sample.shape[:2]
    assert seq % TRUNK_TILE == 0 and tdec == SUBLANES and cache_k.shape[2] == WINDOW

    w = w_in[0]
    w_main = jnp.concatenate([
        w[:, OFF_ZA:OFF_ZS], w[:, OFF_DT:], w[:, :OFF_Q], w[:, OFF_V:OFF_ZA], w[:, OFF_ZS:OFF_XBC], w[:, OFF_Q:OFF_V],
    ], axis=1).astype(BF16)
    w_dt = jnp.pad(w[:, OFF_XBC:OFF_DT], ((0, 0), (0, LANES - SSD_HEADS))).astype(BF16)
    nw = norm_w[0].reshape(1, D_MODEL)
    w_ap, w_sp, w_o = w_attn_proj[0].astype(BF16), w_ssd_proj[0].astype(BF16), w_out[0].astype(BF16)

    lane = jnp.arange(LANES)
    bd = (lane[:, None] // HEAD_DIM == lane[None, :] // HEAD_DIM).astype(F32) / HEAD_DIM
    bd2 = jnp.concatenate([bd, bd], axis=0).astype(BF16)
    head_of_lane = jnp.arange(SSD_WIDTH) // SSD_HEAD_DIM
    e1 = (lane[:, None] == head_of_lane[None, :]).astype(BF16)
    e2 = jnp.concatenate([e1, e1], axis=0)
    tri = (lane[:, None] >= lane[None, :]).astype(BF16)
    qnw2 = jnp.tile(q_norm_w[0].reshape(1, HEAD_DIM), (1, LANES // HEAD_DIM))
    knw2 = jnp.tile(k_norm_w[0].reshape(1, HEAD_DIM), (1, LANES // HEAD_DIM))
    ssd_consts = (conv_w[0], conv_b[0].reshape(1, CONV_DIM), _pad_lanes(dt_bias[0]), _pad_lanes(A_log[0]),
                  jnp.repeat(D_skip[0], SSD_HEAD_DIM).reshape(1, SSD_WIDTH), ssd_norm_w[0].reshape(1, SSD_WIDTH),
                  e2, tri)
    sink_vec = sinks[0].astype(F32)

    xp2 = x_prompt.reshape(bp * seq, D_MODEL)
    tables_p = _rope_tables(jnp.arange(seq, dtype=jnp.int32))
    gates_p, oa_p, os_p, kwin_p, vwin_p, conv_p, ssm_p = _trunk(
        xp2, sink_vec, nw, w_main, w_dt, tables_p, qnw2, knw2, bd2, ssd_consts, bp, seq)
    y_p = _outproj(xp2, gates_p, 0, oa_p, os_p, w_ap, w_sp, w_o, 512).reshape(bp, seq, D_MODEL)

    xs2 = x_sample.reshape(bs * tdec, D_MODEL)
    seqs_per_step = 4
    proj_s, dt_s = _inproj(xs2, nw, w_main, w_dt, 256, F32)
    pos_s = PAST_LEN + jnp.arange(tdec, dtype=jnp.int32)
    tables_s = tuple(jnp.tile(tb, (seqs_per_step, 1)) for tb in _rope_tables(pos_s))
    ck = cache_k[0].reshape(bs, WINDOW, KV_WIDTH)
    cv = cache_v[0].reshape(bs, WINDOW, KV_WIDTH)
    oa_s, kwin_s, vwin_s = _attention_decode(proj_s, sink_vec, tables_s, qnw2, knw2, bd2, (ck, cv),
                                             bs // seqs_per_step, seqs_per_step, tdec)
    h0 = state_ssm[0].reshape(bs, SSD_WIDTH, D_STATE)
    os_s, conv_s, ssm_s = _ssd_decode(proj_s, dt_s, ssd_consts, (state_conv[0], h0), bs, tdec, 2)
    y_s = _outproj(xs2, proj_s, COL_G // (2 * D_MODEL), oa_s, os_s, w_ap, w_sp, w_o, 256).reshape(bs, tdec, D_MODEL)

    win = lambda a, n: a.reshape(1, n, WINDOW, N_KV_HEADS, HEAD_DIM)
    ssm = lambda a, n: a.reshape(1, n, SSD_HEADS, SSD_HEAD_DIM, D_STATE)
    return (y_p, y_s, win(kwin_p, bp), win(vwin_p, bp), conv_p[None], ssm(ssm_p, bp),
            win(kwin_s, bs), win(vwin_s, bs), conv_s[None], ssm(ssm_s, bs))
```

```python
import functools
import types

import jax
import jax.numpy as jnp
from jax import lax
from jax.experimental import pallas as pl
from jax.experimental.pallas import tpu as pltpu

F32 = jnp.float32
BF16 = jnp.bfloat16

D_MODEL = 1024
N_HEADS = 16
N_KV_HEADS = 4
HEAD_DIM = 64
KV_GROUP = N_HEADS // N_KV_HEADS
ATTN_WIDTH = N_HEADS * HEAD_DIM
KV_WIDTH = N_KV_HEADS * HEAD_DIM
WINDOW = 128
ROPE_DIM = HEAD_DIM // 4
ROPE_THETA = 500000.0
PAST_LEN = 16384
SSD_WIDTH = 2 * D_MODEL
SSD_HEAD_DIM = 64
SSD_HEADS = SSD_WIDTH // SSD_HEAD_DIM
SSD_GROUPS = 4
SSD_HPG = SSD_HEADS // SSD_GROUPS
D_STATE = 128
CONV_WIDTH = 4
CONV_DIM = SSD_WIDTH + 2 * SSD_GROUPS * D_STATE
SSD_CHUNK = 128
EPS = 1e-6

OFF_Q = ATTN_WIDTH
OFF_K = OFF_Q + KV_WIDTH
OFF_V = OFF_K + KV_WIDTH
OFF_ZA = OFF_V + ATTN_WIDTH
OFF_ZS = OFF_ZA + SSD_WIDTH
OFF_XBC = OFF_ZS + CONV_DIM
OFF_DT = OFF_XBC + SSD_HEADS

LANES = 128
SUBLANES = 8
VMEM_LIMIT_BYTES = 56 * 1024 * 1024

COL_ZS = 0
COL_G = COL_ZS + SSD_WIDTH
COL_Q = COL_G + 2 * D_MODEL
COL_ZA = COL_Q + ATTN_WIDTH
COL_XBC = COL_ZA + ATTN_WIDTH
COL_KV = COL_XBC + CONV_DIM
N_MAIN = COL_KV + 2 * KV_WIDTH
PROJ_TILE_N = 512
TRUNK_TILE = 2 * SSD_CHUNK
GROUP_WIDTH = SSD_WIDTH // SSD_GROUPS
HALF = LANES // 2
NEG_BIG = -1e30
LOG2E = 1.4426950408889634


def _sigmoid(x):
    return 0.5 * jnp.tanh(0.5 * x) + 0.5


def _silu(x):
    hx = 0.5 * x
    return hx * jnp.tanh(hx) + hx


def _split2(v):
    hi = v.astype(BF16)
    lo = (v - hi.astype(F32)).astype(BF16)
    return hi, lo


def _drain(gen):
    for _ in gen:
        pass


def _interleave(main, side, every):
    live = {"main": True, "side": True}

    def step(gen, key):
        if live[key]:
            try:
                next(gen)
            except StopIteration:
                live[key] = False

    while live["main"] or live["side"]:
        for _ in range(every):
            step(main, "main")
        step(side, "side")


def _inproj_gen(x_ref, nw_ref, w_ref, wdt_ref, store_cols, store_dt, h_scr):
    x = x_ref[...]
    ms = jnp.mean(x * x, axis=-1, keepdims=True)
    h_scr[...] = (x * lax.rsqrt(ms + EPS) * nw_ref[...]).astype(BF16)
    yield
    for j in range(N_MAIN // PROJ_TILE_N):
        sl = slice(j * PROJ_TILE_N, (j + 1) * PROJ_TILE_N)
        store_cols(j, jnp.dot(h_scr[...], w_ref[:, sl], preferred_element_type=F32))
        yield
    store_dt(jnp.dot(h_scr[...], wdt_ref[...], preferred_element_type=F32))
    yield


def _inproj_body(x_ref, nw_ref, w_ref, wdt_ref, out_ref, dt_ref, h_scr):
    def store_cols(j, val):
        out_ref[:, j * PROJ_TILE_N:(j + 1) * PROJ_TILE_N] = val.astype(out_ref.dtype)

    def store_dt(val):
        dt_ref[...] = val

    _drain(_inproj_gen(x_ref, nw_ref, w_ref, wdt_ref, store_cols, store_dt, h_scr))


def _inproj(x2d, norm_w, w_main, w_dt, tm, out_dtype):
    m = x2d.shape[0]
    return pl.pallas_call(
        _inproj_body,
        grid=(m // tm,),
        in_specs=[
            pl.BlockSpec((tm, D_MODEL), lambda i: (i, 0)),
            pl.BlockSpec((1, D_MODEL), lambda i: (0, 0)),
            pl.BlockSpec((D_MODEL, N_MAIN), lambda i: (0, 0), pipeline_mode=pl.Buffered(1)),
            pl.BlockSpec((D_MODEL, LANES), lambda i: (0, 0), pipeline_mode=pl.Buffered(1)),
        ],
        out_specs=[
            pl.BlockSpec((tm, N_MAIN), lambda i: (i, 0)),
            pl.BlockSpec((tm, LANES), lambda i: (i, 0)),
        ],
        out_shape=[
            jax.ShapeDtypeStruct((m, N_MAIN), out_dtype),
            jax.ShapeDtypeStruct((m, LANES), F32),
        ],
        scratch_shapes=[pltpu.VMEM((tm, D_MODEL), BF16)],
        compiler_params=pltpu.CompilerParams(
            dimension_semantics=("arbitrary",), vmem_limit_bytes=VMEM_LIMIT_BYTES),
        name="inproj",
    )(x2d, norm_w, w_main, w_dt)


def _outproj_body(x_ref, g_ref, oa_ref, os_ref, wa_ref, ws_ref, wo_ref, y_ref):
    pa = jnp.dot(oa_ref[...].astype(BF16), wa_ref[...], preferred_element_type=F32)
    ps = jnp.dot(os_ref[...].astype(BF16), ws_ref[...], preferred_element_type=F32)
    g = g_ref[...].astype(F32)
    merged = _sigmoid(g[:, :D_MODEL]) * pa + _sigmoid(g[:, D_MODEL:]) * ps
    y_ref[...] = x_ref[...] + jnp.dot(merged.astype(BF16), wo_ref[...], preferred_element_type=F32)


def _outproj(x2d, gates, gate_col_block, o_a, o_s, w_ap, w_sp, w_o, tm):
    m = x2d.shape[0]
    const = lambda i: (0, 0)
    return pl.pallas_call(
        _outproj_body,
        grid=(m // tm,),
        in_specs=[
            pl.BlockSpec((tm, D_MODEL), lambda i: (i, 0)),
            pl.BlockSpec((tm, 2 * D_MODEL), lambda i: (i, gate_col_block)),
            pl.BlockSpec((tm, ATTN_WIDTH), lambda i: (i, 0)),
            pl.BlockSpec((tm, SSD_WIDTH), lambda i: (i, 0)),
            pl.BlockSpec((ATTN_WIDTH, D_MODEL), const, pipeline_mode=pl.Buffered(1)),
            pl.BlockSpec((SSD_WIDTH, D_MODEL), const, pipeline_mode=pl.Buffered(1)),
            pl.BlockSpec((D_MODEL, D_MODEL), const, pipeline_mode=pl.Buffered(1)),
        ],
        out_specs=pl.BlockSpec((tm, D_MODEL), lambda i: (i, 0)),
        out_shape=jax.ShapeDtypeStruct((m, D_MODEL), F32),
        compiler_params=pltpu.CompilerParams(
            dimension_semantics=("arbitrary",), vmem_limit_bytes=VMEM_LIMIT_BYTES),
        name="outproj",
    )(x2d, gates, o_a, o_s, w_ap, w_sp, w_o)


def _norm_rope(xc, nw, bd2, cos_t, sin_lo, sin_hi):
    hi, lo = _split2(xc * xc)
    ms = jnp.dot(jnp.concatenate([hi, lo], axis=1), bd2, preferred_element_type=F32)
    xn = xc * lax.rsqrt(ms + EPS) * nw
    return xn * cos_t + pltpu.roll(xn, LANES - ROPE_DIM // 2, 1) * sin_lo + pltpu.roll(xn, ROPE_DIM // 2, 1) * sin_hi


def _dup_half(chunk, keep_low, lane_lo):
    rolled = pltpu.roll(chunk, HALF, 1)
    return jnp.where(lane_lo, chunk, rolled) if keep_low else jnp.where(lane_lo, rolled, chunk)


def _attn_reset(kdup, vdup):
    for g in range(N_KV_HEADS):
        kdup[0, g, WINDOW:2 * WINDOW, :] = jnp.zeros((WINDOW, LANES), BF16)
        vdup[0, g, WINDOW:2 * WINDOW, 0:LANES] = jnp.zeros((WINDOW, LANES), BF16)
    vdup[:, :, :, LANES:2 * LANES] = jnp.ones((vdup.shape[0], N_KV_HEADS, 2 * WINDOW, LANES), BF16)


def _attn_gen(io, sinks_ref, qnw_ref, knw_ref, bd2_ref, kdup, vdup, o_scr, *, seqs, t, decode, prev_valid,
              own_half):
    tp = max(t, 2 * SUBLANES)
    nkeys = 2 * WINDOW
    own_lo, prev_lo = own_half * WINDOW, (1 - own_half) * WINDOW
    bd2 = bd2_ref[...]
    cos_t, sin_lo, sin_hi = io.tables()
    lane_lo = lax.broadcasted_iota(jnp.int32, (1, LANES), 1) < HALF

    kv = io.kv().astype(F32)
    k_rot = [_norm_rope(kv[:, c * LANES:(c + 1) * LANES], knw_ref[...], bd2, cos_t, sin_lo, sin_hi)
             for c in range(KV_WIDTH // LANES)]
    v_raw = [kv[:, KV_WIDTH + c * LANES:KV_WIDTH + (c + 1) * LANES] for c in range(KV_WIDTH // LANES)]
    yield
    q_all = io.q().astype(F32)
    q_rot = []
    for c in range(ATTN_WIDTH // LANES):
        q_rot.append(_norm_rope(q_all[:, c * LANES:(c + 1) * LANES], qnw_ref[...], bd2, cos_t, sin_lo, sin_hi)
                     * (HEAD_DIM ** -0.5))
        if c % 2 == 1:
            yield

    if decode:
        vdup[:, :, :, LANES:2 * LANES] = jnp.ones((seqs, N_KV_HEADS, nkeys, LANES), BF16)

    own = (lax.broadcasted_iota(jnp.int32, (t, WINDOW), 1) <= lax.broadcasted_iota(jnp.int32, (t, WINDOW), 0))
    prev_bias = None if prev_valid is None else jnp.where(prev_valid, 0.0, NEG_BIG)

    def pad_rows(a):
        if tp == t:
            return a
        return jnp.concatenate([a, jnp.zeros((tp - t, a.shape[1]), a.dtype)], axis=0)

    for si in range(seqs):
        rs = slice(si * t, (si + 1) * t)
        for g in range(N_KV_HEADS):
            c, low = g // 2, (g % 2 == 0)
            kdup[si, g, own_lo:own_lo + tp, :] = pad_rows(_dup_half(k_rot[c][rs], low, lane_lo)).astype(BF16)
            vdup[si, g, own_lo:own_lo + tp, 0:LANES] = pad_rows(_dup_half(v_raw[c][rs], low, lane_lo)).astype(BF16)
            if decode:
                kdup[si, g, prev_lo:prev_lo + WINDOW, :] = _dup_half(
                    io.ck(si)[:, c * LANES:(c + 1) * LANES], low, lane_lo).astype(BF16)
                vdup[si, g, prev_lo:prev_lo + WINDOW, 0:LANES] = _dup_half(
                    io.cv(si)[:, c * LANES:(c + 1) * LANES], low, lane_lo).astype(BF16)
                kdup[si, g, own_lo + tp:own_lo + WINDOW, :] = jnp.zeros((WINDOW - tp, LANES), BF16)
                vdup[si, g, own_lo + tp:own_lo + WINDOW, 0:LANES] = jnp.zeros((WINDOW - tp, LANES), BF16)
        yield
        scores = []
        for g in range(N_KV_HEADS):
            q_stack = jnp.concatenate(
                [jnp.where(lane_lo if j % 2 == 0 else jnp.logical_not(lane_lo), q_rot[j // 2][rs], 0.0)
                 for j in range(KV_GROUP * g, KV_GROUP * (g + 1))], axis=0).astype(BF16)
            scores.append(lax.dot_general(q_stack, kdup[si, g], (((1,), (1,)), ((), ())),
                                          preferred_element_type=F32))
            yield
        probs, sink_terms = [], []
        for j in range(N_HEADS):
            s = scores[j // KV_GROUP]
            r0 = (j % KV_GROUP) * t
            s_prev = s[r0:r0 + t, prev_lo:prev_lo + WINDOW]
            if prev_bias is not None:
                s_prev = s_prev + prev_bias
            sj = jnp.where(own, s[r0:r0 + t, own_lo:own_lo + WINDOW], s_prev)
            sink = sinks_ref[j]
            mx = jnp.maximum(jnp.max(sj, axis=-1, keepdims=True), sink)
            p = jnp.exp(sj - mx)
            sink_terms.append(jnp.exp(sink - mx))
            halves = [jnp.where(own, 0.0, p), jnp.where(own, p, 0.0)]
            probs.append(jnp.concatenate(halves if own_half == 1 else halves[::-1], axis=1))
            if j % KV_GROUP == KV_GROUP - 1:
                yield
        for g in range(N_KV_HEADS):
            p_all = jnp.concatenate(probs[KV_GROUP * g:KV_GROUP * (g + 1)], axis=0).astype(BF16)
            o_aug = jnp.dot(p_all, vdup[si, g], preferred_element_type=F32)
            o = [o_aug[jj * t:(jj + 1) * t, 0:LANES]
                 / (o_aug[jj * t:(jj + 1) * t, LANES:2 * LANES] + sink_terms[KV_GROUP * g + jj])
                 for jj in range(KV_GROUP)]
            for half in range(2):
                o_scr[rs, (2 * g + half) * LANES:(2 * g + half + 1) * LANES] = jnp.where(
                    lane_lo, o[2 * half], o[2 * half + 1])
            yield

    io.oa(o_scr[...] * _silu(io.za().astype(F32)))
    yield

    k_new = jnp.concatenate(k_rot, axis=1)
    v_new = jnp.concatenate(v_raw, axis=1)
    if decode:
        for si in range(seqs):
            io.kwin(si, 0, WINDOW - t, io.ck(si)[t:WINDOW, :])
            io.vwin(si, 0, WINDOW - t, io.cv(si)[t:WINDOW, :])
            io.kwin(si, WINDOW - t, WINDOW, k_new[si * t:(si + 1) * t])
            io.vwin(si, WINDOW - t, WINDOW, v_new[si * t:(si + 1) * t])
    else:
        io.kwin(0, 0, WINDOW, k_new)
        io.vwin(0, 0, WINDOW, v_new)
    yield


def _attn_scratch(seqs, rows):
    return [
        pltpu.VMEM((seqs, N_KV_HEADS, 2 * WINDOW, LANES), BF16),
        pltpu.VMEM((seqs, N_KV_HEADS, 2 * WINDOW, 2 * LANES), BF16),
        pltpu.VMEM((rows, ATTN_WIDTH), F32),
    ]


def _attn_decode_body(sinks_ref, q_ref, kv_ref, za_ref, cos_ref, slo_ref, shi_ref, qnw_ref, knw_ref, bd2_ref,
                      ck_ref, cv_ref, oa_ref, kwin_ref, vwin_ref, kdup, vdup, o_scr, *, seqs, t):
    def store_oa(val):
        oa_ref[...] = val.astype(oa_ref.dtype)

    def store_kwin(si, lo, hi, val):
        kwin_ref[si, lo:hi, :] = val

    def store_vwin(si, lo, hi, val):
        vwin_ref[si, lo:hi, :] = val

    io = types.SimpleNamespace(
        q=lambda: q_ref[...], kv=lambda: kv_ref[...], za=lambda: za_ref[...],
        tables=lambda: (cos_ref[...], slo_ref[...], shi_ref[...]),
        oa=store_oa, kwin=store_kwin, vwin=store_vwin, ck=lambda si: ck_ref[si], cv=lambda si: cv_ref[si])
    _drain(_attn_gen(io, sinks_ref, qnw_ref, knw_ref, bd2_ref, kdup, vdup, o_scr,
                     seqs=seqs, t=t, decode=True, prev_valid=None, own_half=1))


def _attention_decode(proj, sinks, tables, qnw2, knw2, bd2, caches, nsteps, seqs, t):
    rows = seqs * t
    m = proj.shape[0]
    const = lambda s: (0, 0)
    in_specs = [
        pl.BlockSpec(memory_space=pltpu.SMEM),
        pl.BlockSpec((rows, ATTN_WIDTH), lambda s: (s, COL_Q // ATTN_WIDTH)),
        pl.BlockSpec((rows, 2 * KV_WIDTH), lambda s: (s, COL_KV // (2 * KV_WIDTH))),
        pl.BlockSpec((rows, ATTN_WIDTH), lambda s: (s, COL_ZA // ATTN_WIDTH)),
        pl.BlockSpec((rows, LANES), const),
        pl.BlockSpec((rows, LANES), const),
        pl.BlockSpec((rows, LANES), const),
        pl.BlockSpec((1, LANES), const),
        pl.BlockSpec((1, LANES), const),
        pl.BlockSpec((2 * LANES, LANES), const),
        pl.BlockSpec((seqs, WINDOW, KV_WIDTH), lambda s: (s, 0, 0)),
        pl.BlockSpec((seqs, WINDOW, KV_WIDTH), lambda s: (s, 0, 0)),
    ]
    return pl.pallas_call(
        functools.partial(_attn_decode_body, seqs=seqs, t=t),
        grid=(nsteps,),
        in_specs=in_specs,
        out_specs=[
            pl.BlockSpec((rows, ATTN_WIDTH), lambda s: (s, 0)),
            pl.BlockSpec((seqs, WINDOW, KV_WIDTH), lambda s: (s, 0, 0)),
            pl.BlockSpec((seqs, WINDOW, KV_WIDTH), lambda s: (s, 0, 0)),
        ],
        out_shape=[
            jax.ShapeDtypeStruct((m, ATTN_WIDTH), F32),
            jax.ShapeDtypeStruct((nsteps * seqs, WINDOW, KV_WIDTH), F32),
            jax.ShapeDtypeStruct((nsteps * seqs, WINDOW, KV_WIDTH), F32),
        ],
        scratch_shapes=_attn_scratch(seqs, rows),
        compiler_params=pltpu.CompilerParams(
            dimension_semantics=("arbitrary",), vmem_limit_bytes=VMEM_LIMIT_BYTES),
        name="attn_decode",
    )(sinks, proj, proj, proj, *tables, qnw2, knw2, bd2, *caches)


def _ssd_gen(io, cw_ref, cb_ref, dtb_ref, alog_ref, dsk_ref, nw_ref, e2_ref, tri_ref,
             xp, xs_scr, bc_scr, acs_scr, acst_scr, dtt_scr, ht, y_scr, *, q, decode):
    lc = SSD_CHUNK
    tail = CONV_WIDTH - 1
    base = SUBLANES

    slab = 4 * LANES
    for cs in range(CONV_DIM // slab):
        cl = slice(cs * slab, (cs + 1) * slab)
        if decode:
            acc = cb_ref[:, cl] + cw_ref[0:1, cl] * xp[base - tail:base - tail + q, cl]
            for j in range(1, CONV_WIDTH):
                acc = acc + cw_ref[j:j + 1, cl] * xp[base - tail + j:base - tail + j + q, cl]
        else:
            xe = jnp.concatenate([xp[:, cl], io.xbc(cl).astype(F32)], axis=0)
            acc = cw_ref[0:1, cl] * xe
            for j in range(1, CONV_WIDTH):
                acc = pltpu.roll(acc, 1, 0) + cw_ref[j:j + 1, cl] * xe
            acc = acc[SUBLANES:] + cb_ref[:, cl]
        act = _silu(acc)
        if cs < SSD_WIDTH // slab:
            xs_scr[0:q, cl] = act
        else:
            bc_scr[0:q, cs * slab - SSD_WIDTH:(cs + 1) * slab - SSD_WIDTH] = act
        yield
    if not decode:
        xp[...] = io.xbc_last_rows().astype(F32)[SUBLANES:2 * SUBLANES]

    dt_in = io.dt()
    if q < lc:
        dt_in = jnp.concatenate([dt_in, jnp.zeros((lc - q, LANES), F32)], axis=0)
    xdt = dt_in + dtb_ref[...]
    dt = jnp.maximum(xdt, 0.0) + jnp.log1p(jnp.exp(-jnp.abs(xdt)))
    if q < lc:
        dt = jnp.where(lax.broadcasted_iota(jnp.int32, (lc, LANES), 0) < q, dt, 0.0)
    a = dt * (-jnp.exp(alog_ref[...]))
    a_hi = a.astype(BF16)
    a_r1 = a - a_hi.astype(F32)
    a_mid = a_r1.astype(BF16)
    a_lo = (a_r1 - a_mid.astype(F32)).astype(BF16)
    acs3 = jnp.dot(tri_ref[...], jnp.concatenate([a_hi, a_mid, a_lo], axis=1), preferred_element_type=F32)
    acs = acs3[:, 0:LANES] + acs3[:, LANES:2 * LANES] + acs3[:, 2 * LANES:3 * LANES]
    acs2 = acs * LOG2E
    acs_scr[...] = acs2
    acst_scr[...] = acs2.T
    dtt_scr[...] = dt.T
    yield

    def expand(v):
        hi, lo = _split2(v)
        return jnp.dot(jnp.concatenate([hi, lo], axis=1), e2_ref[...], preferred_element_type=F32)

    both = expand(jnp.concatenate([jnp.exp(acs[0:q]), (jnp.exp(acs[lc - 1:lc, :] - acs) * dt)[0:q]], axis=0))
    e_off, w_state = both[0:q], both[q:2 * q]
    chunk_decay = e_off[q - 1:q, :]
    yield

    causal = (lax.broadcasted_iota(jnp.int32, (q, lc), 0) >= lax.broadcasted_iota(jnp.int32, (q, lc), 1))
    lane_lo = lax.broadcasted_iota(jnp.int32, (1, LANES), 1) < HALF
    nbc = SSD_GROUPS * D_STATE
    pairs_per_group = SSD_HPG // 2
    gls = [slice(g * GROUP_WIDTH, (g + 1) * GROUP_WIDTH) for g in range(SSD_GROUPS)]
    b_gs = [bc_scr[:, g * D_STATE:(g + 1) * D_STATE] for g in range(SSD_GROUPS)]
    c_gs = [bc_scr[0:q, nbc + g * D_STATE:nbc + (g + 1) * D_STATE].astype(BF16) for g in range(SSD_GROUPS)]
    cbs = [lax.dot_general(c_gs[g], b_gs[g].astype(BF16), (((1,), (1,)), ((), ())), preferred_element_type=F32)
           for g in range(SSD_GROUPS)]
    yield
    y_offs = []
    for g in range(SSD_GROUPS):
        y_offs.append(jnp.dot(c_gs[g], ht[g].astype(BF16), preferred_element_type=F32) * e_off[0:q, gls[g]])
        yield
    for g in range(SSD_GROUPS):
        xw = xs_scr[0:q, gls[g]] * w_state[:, gls[g]]
        if q < lc:
            xw = jnp.concatenate([xw, jnp.zeros((lc - q, GROUP_WIDTH), F32)], axis=0)
        xw = xw.astype(BF16)
        ht[g] = ht[g] * chunk_decay[:, gls[g]] + jnp.dot(b_gs[g].T.astype(BF16), xw, preferred_element_type=F32)
        yield
    w_pairs = []
    for pair in range(SSD_HEADS // 2):
        w_parts = []
        for h in (2 * pair, 2 * pair + 1):
            seg2 = acs_scr[0:q, h:h + 1] - acst_scr[h:h + 1, :]
            decay = jnp.exp2(jnp.where(causal, seg2, -jnp.inf))
            w_parts.append(cbs[pair // pairs_per_group] * decay * dtt_scr[h:h + 1, :])
        w_pairs.append(jnp.concatenate(w_parts, axis=1).astype(BF16))
        if pair % 2 == 1:
            yield
    y_diags = []
    for pair in range(SSD_HEADS // 2):
        x_pair = xs_scr[:, pair * LANES:(pair + 1) * LANES]
        x_bd = jnp.concatenate([jnp.where(lane_lo, x_pair, 0.0), jnp.where(lane_lo, 0.0, x_pair)],
                               axis=0).astype(BF16)
        y_diags.append(jnp.dot(w_pairs[pair], x_bd, preferred_element_type=F32))
        if pair % 4 == 3:
            yield
    for pair in range(SSD_HEADS // 2):
        g, pr = divmod(pair, pairs_per_group)
        pl_ = slice(pair * LANES, (pair + 1) * LANES)
        y_pair = y_diags[pair] + y_offs[g][:, pr * LANES:(pr + 1) * LANES] + dsk_ref[:, pl_] * xs_scr[0:q, pl_]
        y_scr[:, pl_] = y_pair * _silu(io.zs(pl_).astype(F32))
        if pair % 4 == 3:
            yield
    for g in range(SSD_GROUPS):
        yg = y_scr[:, gls[g]]
        ms = jnp.mean(yg * yg, axis=-1, keepdims=True)
        io.os(gls[g], yg * lax.rsqrt(ms + EPS) * nw_ref[:, gls[g]])
        yield


def _ssd_write_state(hout_ref, ht):
    for g in range(SSD_GROUPS):
        hout_ref[0, g * GROUP_WIDTH:(g + 1) * GROUP_WIDTH, :] = ht[g].T


def _ssd_scratch(q, staged_rows):
    return [
        pltpu.VMEM((SUBLANES + staged_rows, CONV_DIM), F32),
        pltpu.VMEM((SSD_CHUNK, SSD_WIDTH), F32),
        pltpu.VMEM((SSD_CHUNK, 2 * SSD_GROUPS * D_STATE), F32),
        pltpu.VMEM((SSD_CHUNK, LANES), F32),
        pltpu.VMEM((LANES, SSD_CHUNK), F32),
        pltpu.VMEM((LANES, SSD_CHUNK), F32),
        pltpu.VMEM((SSD_GROUPS, D_STATE, GROUP_WIDTH), F32),
        pltpu.VMEM((q, SSD_WIDTH), F32),
    ]


def _ssd_decode_body(xbc_ref, zs_ref, dt_ref, cw_ref, cb_ref, dtb_ref, alog_ref, dsk_ref, nw_ref, e2_ref, tri_ref,
                     cst_ref, h0_ref, os_ref, cout_ref, hout_ref, *scratch, q, seqs):
    tail = CONV_WIDTH - 1
    base = SUBLANES
    per_seq = len(scratch) // seqs
    gens = []
    for si in range(seqs):
        xp, xs_scr, bc_scr, acs_scr, acst_scr, dtt_scr, ht, y_scr = scratch[si * per_seq:(si + 1) * per_seq]
        rows = slice(si * q, (si + 1) * q)

        @pl.when(pl.program_id(0) == 0)
        def _(xs_scr=xs_scr, bc_scr=bc_scr):
            xs_scr[q:SSD_CHUNK, :] = jnp.zeros((SSD_CHUNK - q, SSD_WIDTH), F32)
            bc_scr[q:SSD_CHUNK, :] = jnp.zeros((SSD_CHUNK - q, 2 * SSD_GROUPS * D_STATE), F32)

        xp[base - tail:base, :] = cst_ref[si]
        xp[base:base + q, :] = xbc_ref[rows, :].astype(F32)
        for g in range(SSD_GROUPS):
            ht[g] = h0_ref[si, g * GROUP_WIDTH:(g + 1) * GROUP_WIDTH, :].T

        def store_os(cols, val, rows=rows):
            os_ref[rows, cols] = val.astype(os_ref.dtype)

        io = types.SimpleNamespace(zs=lambda cols, rows=rows: zs_ref[rows, cols], dt=lambda rows=rows: dt_ref[rows, :],
                                   os=store_os)
        gens.append(_ssd_gen(io, cw_ref, cb_ref, dtb_ref, alog_ref, dsk_ref, nw_ref, e2_ref, tri_ref,
                             xp, xs_scr, bc_scr, acs_scr, acst_scr, dtt_scr, ht, y_scr, q=q, decode=True))
    while gens:
        for gen in list(gens):
            try:
                next(gen)
            except StopIteration:
                gens.remove(gen)
    for si in range(seqs):
        xp, ht = scratch[si * per_seq], scratch[si * per_seq + 6]
        cout_ref[si] = xp[base + q - tail:base + q, :]
        for g in range(SSD_GROUPS):
            hout_ref[si, g * GROUP_WIDTH:(g + 1) * GROUP_WIDTH, :] = ht[g].T


def _ssd_const_specs(const):
    return [
        pl.BlockSpec((CONV_WIDTH, CONV_DIM), const),
        pl.BlockSpec((1, CONV_DIM), const),
        pl.BlockSpec((1, LANES), const),
        pl.BlockSpec((1, LANES), const),
        pl.BlockSpec((1, SSD_WIDTH), const),
        pl.BlockSpec((1, SSD_WIDTH), const),
        pl.BlockSpec((2 * LANES, SSD_WIDTH), const),
        pl.BlockSpec((SSD_CHUNK, SSD_CHUNK), const),
    ]


def _ssd_decode(proj, dt_raw, consts, states, nseq, q, seqs):
    m = proj.shape[0]
    rows = seqs * q
    const = lambda s: (0, 0)
    in_specs = [
        pl.BlockSpec((rows, CONV_DIM), lambda s: (s, COL_XBC // CONV_DIM)),
        pl.BlockSpec((rows, SSD_WIDTH), lambda s: (s, COL_ZS // SSD_WIDTH)),
        pl.BlockSpec((rows, LANES), lambda s: (s, 0)),
        *_ssd_const_specs(const),
        pl.BlockSpec((seqs, CONV_WIDTH - 1, CONV_DIM), lambda s: (s, 0, 0)),
        pl.BlockSpec((seqs, SSD_WIDTH, D_STATE), lambda s: (s, 0, 0)),
    ]
    return pl.pallas_call(
        functools.partial(_ssd_decode_body, q=q, seqs=seqs),
        grid=(nseq // seqs,),
        in_specs=in_specs,
        out_specs=[
            pl.BlockSpec((rows, SSD_WIDTH), lambda s: (s, 0)),
            pl.BlockSpec((seqs, CONV_WIDTH - 1, CONV_DIM), lambda s: (s, 0, 0)),
            pl.BlockSpec((seqs, SSD_WIDTH, D_STATE), lambda s: (s, 0, 0)),
        ],
        out_shape=[
            jax.ShapeDtypeStruct((m, SSD_WIDTH), F32),
            jax.ShapeDtypeStruct((nseq, CONV_WIDTH - 1, CONV_DIM), F32),
            jax.ShapeDtypeStruct((nseq, SSD_WIDTH, D_STATE), F32),
        ],
        scratch_shapes=_ssd_scratch(q, q) * seqs,
        compiler_params=pltpu.CompilerParams(
            dimension_semantics=("arbitrary",), vmem_limit_bytes=VMEM_LIMIT_BYTES),
        name="ssd_decode",
    )(proj, proj, dt_raw, *consts, *states)


def _trunk_body(sinks_ref, x_ref, nw_ref, w_ref, wdt_ref, cos_ref, slo_ref, shi_ref, qnw_ref, knw_ref, bd2_ref,
                cw_ref, cb_ref, dtb_ref, alog_ref, dsk_ref, snw_ref, e2_ref, tri_ref,
                g_ref, oa_ref, os_ref, kwin_ref, vwin_ref, cout_ref, hout_ref,
                proj_a, proj_b, dt_a, dt_b, h_scr, kdup, vdup, o_scr,
                xp, xs_scr, bc_scr, acs_scr, acst_scr, dtt_scr, ht, y_scr, *, tiles_per_seq):
    j = pl.program_id(0)
    tile_in_seq = (j + tiles_per_seq - 1) % tiles_per_seq
    seq_start = tile_in_seq == 0
    chunks = TRUNK_TILE // SSD_CHUNK
    assert chunks % 2 == 0, "chunk parity picks the key/value scratch half across tiles"
    g_tiles = range(COL_G // PROJ_TILE_N, COL_Q // PROJ_TILE_N)

    @pl.when(j == 0)
    def _():
        proj_b[...] = jnp.zeros(proj_b.shape, proj_b.dtype)
        dt_b[...] = jnp.zeros(dt_b.shape, F32)

    @pl.when(seq_start | (j == 0))
    def _():
        _attn_reset(kdup, vdup)
        xp[...] = jnp.zeros(xp.shape, F32)
        ht[...] = jnp.zeros(ht.shape, F32)

    def step(wbuf, wdt, rbuf, rdt):
        def store_cols(jn, val):
            if jn in g_tiles:
                g_ref[:, (jn - g_tiles[0]) * PROJ_TILE_N:(jn - g_tiles[0] + 1) * PROJ_TILE_N] = val.astype(g_ref.dtype)
            else:
                wbuf[:, jn * PROJ_TILE_N:(jn + 1) * PROJ_TILE_N] = val.astype(wbuf.dtype)

        def store_dt(val):
            wdt[...] = val

        producer = _inproj_gen(x_ref, nw_ref, w_ref, wdt_ref, store_cols, store_dt, h_scr)

        def consumer():
            for c in range(chunks):
                rows = slice(c * SSD_CHUNK, (c + 1) * SSD_CHUNK)

                def store_oa(val, rows=rows):
                    oa_ref[rows, :] = val.astype(oa_ref.dtype)

                def store_kwin(si, lo, hi, val):
                    kwin_ref[0, lo:hi, :] = val

                def store_vwin(si, lo, hi, val):
                    vwin_ref[0, lo:hi, :] = val

                def store_os(cols, val, rows=rows):
                    os_ref[rows, cols] = val.astype(os_ref.dtype)

                attn_io = types.SimpleNamespace(
                    q=lambda rows=rows: rbuf[rows, COL_Q:COL_Q + ATTN_WIDTH],
                    kv=lambda rows=rows: rbuf[rows, COL_KV:COL_KV + 2 * KV_WIDTH],
                    za=lambda rows=rows: rbuf[rows, COL_ZA:COL_ZA + ATTN_WIDTH],
                    tables=lambda rows=rows: (cos_ref[rows, :], slo_ref[rows, :], shi_ref[rows, :]),
                    oa=store_oa, kwin=store_kwin, vwin=store_vwin)
                ssd_io = types.SimpleNamespace(
                    xbc=lambda cols, rows=rows: rbuf[rows, COL_XBC + cols.start:COL_XBC + cols.stop],
                    xbc_last_rows=lambda c=c: rbuf[(c + 1) * SSD_CHUNK - 2 * SUBLANES:(c + 1) * SSD_CHUNK,
                                                   COL_XBC:COL_XBC + CONV_DIM],
                    zs=lambda cols, rows=rows: rbuf[rows, COL_ZS + cols.start:COL_ZS + cols.stop],
                    dt=lambda rows=rows: rdt[rows, :], os=store_os)
                prev_valid = (tile_in_seq > 0) if c == 0 else True
                attn = _attn_gen(attn_io, sinks_ref, qnw_ref, knw_ref, bd2_ref, kdup, vdup, o_scr,
                                 seqs=1, t=WINDOW, decode=False, prev_valid=prev_valid, own_half=c % 2)
                ssd = _ssd_gen(ssd_io, cw_ref, cb_ref, dtb_ref, alog_ref, dsk_ref, snw_ref, e2_ref, tri_ref,
                               xp, xs_scr, bc_scr, acs_scr, acst_scr, dtt_scr, ht, y_scr, q=SSD_CHUNK, decode=False)
                live = [attn, ssd]
                while live:
                    for gen in list(live):
                        try:
                            next(gen)
                        except StopIteration:
                            live.remove(gen)
                        yield

        _interleave(consumer(), producer, every=4)

    pl.when(j % 2 == 0)(lambda: step(proj_a, dt_a, proj_b, dt_b))
    pl.when(j % 2 == 1)(lambda: step(proj_b, dt_b, proj_a, dt_a))

    @pl.when((tile_in_seq == tiles_per_seq - 1) & (j > 0))
    def _():
        cout_ref[0] = xp[SUBLANES - (CONV_WIDTH - 1):SUBLANES, :]
        _ssd_write_state(hout_ref, ht)


def _trunk(x2d, sinks, nw, w_main, w_dt, tables, qnw2, knw2, bd2, ssd_consts, nseq, seq):
    m = x2d.shape[0]
    tiles_per_seq = seq // TRUNK_TILE
    ntiles = m // TRUNK_TILE
    produced = lambda j: jnp.minimum(j, ntiles - 1)
    consumed = lambda j: jnp.maximum(j - 1, 0)
    const = lambda j: (0, 0)
    tab_map = lambda j: ((j + tiles_per_seq - 1) % tiles_per_seq, 0)
    seq_map = lambda j: (consumed(j) // tiles_per_seq, 0, 0)
    in_specs = [
        pl.BlockSpec(memory_space=pltpu.SMEM),
        pl.BlockSpec((TRUNK_TILE, D_MODEL), lambda j: (produced(j), 0)),
        pl.BlockSpec((1, D_MODEL), const),
        pl.BlockSpec((D_MODEL, N_MAIN), const, pipeline_mode=pl.Buffered(1)),
        pl.BlockSpec((D_MODEL, LANES), const, pipeline_mode=pl.Buffered(1)),
        pl.BlockSpec((TRUNK_TILE, LANES), tab_map),
        pl.BlockSpec((TRUNK_TILE, LANES), tab_map),
        pl.BlockSpec((TRUNK_TILE, LANES), tab_map),
        pl.BlockSpec((1, LANES), const),
        pl.BlockSpec((1, LANES), const),
        pl.BlockSpec((2 * LANES, LANES), const),
        *_ssd_const_specs(const),
    ]
    return pl.pallas_call(
        functools.partial(_trunk_body, tiles_per_seq=tiles_per_seq),
        grid=(ntiles + 1,),
        in_specs=in_specs,
        out_specs=[
            pl.BlockSpec((TRUNK_TILE, 2 * D_MODEL), lambda j: (produced(j), 0)),
            pl.BlockSpec((TRUNK_TILE, ATTN_WIDTH), lambda j: (consumed(j), 0)),
            pl.BlockSpec((TRUNK_TILE, SSD_WIDTH), lambda j: (consumed(j), 0)),
            pl.BlockSpec((1, WINDOW, KV_WIDTH), seq_map),
            pl.BlockSpec((1, WINDOW, KV_WIDTH), seq_map),
            pl.BlockSpec((1, CONV_WIDTH - 1, CONV_DIM), seq_map),
            pl.BlockSpec((1, SSD_WIDTH, D_STATE), seq_map),
        ],
        out_shape=[
            jax.ShapeDtypeStruct((m, 2 * D_MODEL), BF16),
            jax.ShapeDtypeStruct((m, ATTN_WIDTH), BF16),
            jax.ShapeDtypeStruct((m, SSD_WIDTH), BF16),
            jax.ShapeDtypeStruct((nseq, WINDOW, KV_WIDTH), F32),
            jax.ShapeDtypeStruct((nseq, WINDOW, KV_WIDTH), F32),
            jax.ShapeDtypeStruct((nseq, CONV_WIDTH - 1, CONV_DIM), F32),
            jax.ShapeDtypeStruct((nseq, SSD_WIDTH, D_STATE), F32),
        ],
        scratch_shapes=[
            pltpu.VMEM((TRUNK_TILE, N_MAIN), BF16),
            pltpu.VMEM((TRUNK_TILE, N_MAIN), BF16),
            pltpu.VMEM((TRUNK_TILE, LANES), F32),
            pltpu.VMEM((TRUNK_TILE, LANES), F32),
            pltpu.VMEM((TRUNK_TILE, D_MODEL), BF16),
            *_attn_scratch(1, WINDOW),
            *_ssd_scratch(SSD_CHUNK, 0),
        ],
        compiler_params=pltpu.CompilerParams(
            dimension_semantics=("arbitrary",), vmem_limit_bytes=VMEM_LIMIT_BYTES),
        name="prompt_trunk",
    )(sinks, x2d, nw, w_main, w_dt, *tables, qnw2, knw2, bd2, *ssd_consts)


def _rope_tables(pos):
    half = ROPE_DIM // 2
    inv_freq = jnp.power(ROPE_THETA, -jnp.arange(half, dtype=F32) * (2.0 / ROPE_DIM))
    ang = pos.astype(F32)[:, None] * inv_freq[None, :]
    cos, sin = jnp.cos(ang), jnp.sin(ang)
    n = pos.shape[0]
    rest = HEAD_DIM - ROPE_DIM
    z_half, z_rest = jnp.zeros((n, half), F32), jnp.zeros((n, rest), F32)
    cos_t = jnp.concatenate([cos, cos, jnp.ones((n, rest), F32)], axis=1)
    sin_lo = jnp.concatenate([-sin, z_half, z_rest], axis=1)
    sin_hi = jnp.concatenate([z_half, sin, z_rest], axis=1)
    return tuple(jnp.tile(tb, (1, LANES // HEAD_DIM)) for tb in (cos_t, sin_lo, sin_hi))


def _pad_lanes(v):
    return jnp.pad(v.reshape(1, -1).astype(F32), ((0, 0), (0, LANES - v.shape[-1])))


def kernel(x_prompt, x_sample, cache_k, cache_v, state_conv, state_ssm, norm_w, w_in, q_norm_w, k_norm_w, sinks,
           conv_w, conv_b, dt_bias, A_log, D_skip, ssd_norm_w, w_attn_proj, w_ssd_proj, w_out):
    depth = norm_w.shape[0]
    assert depth == 1, "single-layer trunk"
    bp, seq = x_prompt.shape[:2]
    bs, tdec = x_sample.shape[:2]
    assert seq % TRUNK_TILE == 0 and tdec == SUBLANES and cache_k.shape[2] == WINDOW

    w = w_in[0]
    w_main = jnp.concatenate([
        w[:, OFF_ZA:OFF_ZS], w[:, OFF_DT:], w[:, :OFF_Q], w[:, OFF_V:OFF_ZA], w[:, OFF_ZS:OFF_XBC], w[:, OFF_Q:OFF_V],
    ], axis=1).astype(BF16)
    w_dt = jnp.pad(w[:, OFF_XBC:OFF_DT], ((0, 0), (0, LANES - SSD_HEADS))).astype(BF16)
    nw = norm_w[0].reshape(1, D_MODEL)
    w_ap, w_sp, w_o = w_attn_proj[0].astype(BF16), w_ssd_proj[0].astype(BF16), w_out[0].astype(BF16)

    lane = jnp.arange(LANES)
    bd = (lane[:, None] // HEAD_DIM == lane[None, :] // HEAD_DIM).astype(F32) / HEAD_DIM
    bd2 = jnp.concatenate([bd, bd], axis=0).astype(BF16)
    head_of_lane = jnp.arange(SSD_WIDTH) // SSD_HEAD_DIM
    e1 = (lane[:, None] == head_of_lane[None, :]).astype(BF16)
    e2 = jnp.concatenate([e1, e1], axis=0)
    tri = (lane[:, None] >= lane[None, :]).astype(BF16)
    qnw2 = jnp.tile(q_norm_w[0].reshape(1, HEAD_DIM), (1, LANES // HEAD_DIM))
    knw2 = jnp.tile(k_norm_w[0].reshape(1, HEAD_DIM), (1, LANES // HEAD_DIM))
    ssd_consts = (conv_w[0], conv_b[0].reshape(1, CONV_DIM), _pad_lanes(dt_bias[0]), _pad_lanes(A_log[0]),
                  jnp.repeat(D_skip[0], SSD_HEAD_DIM).reshape(1, SSD_WIDTH), ssd_norm_w[0].reshape(1, SSD_WIDTH),
                  e2, tri)
    sink_vec = sinks[0].astype(F32)

    xp2 = x_prompt.reshape(bp * seq, D_MODEL)
    tables_p = _rope_tables(jnp.arange(seq, dtype=jnp.int32))
    gates_p, oa_p, os_p, kwin_p, vwin_p, conv_p, ssm_p = _trunk(
        xp2, sink_vec, nw, w_main, w_dt, tables_p, qnw2, knw2, bd2, ssd_consts, bp, seq)
    y_p = _outproj(xp2, gates_p, 0, oa_p, os_p, w_ap, w_sp, w_o, 1024).reshape(bp, seq, D_MODEL)

    xs2 = x_sample.reshape(bs * tdec, D_MODEL)
    seqs_per_step = 8
    proj_s, dt_s = _inproj(xs2, nw, w_main, w_dt, 256, F32)
    pos_s = PAST_LEN + jnp.arange(tdec, dtype=jnp.int32)
    tables_s = tuple(jnp.tile(tb, (seqs_per_step, 1)) for tb in _rope_tables(pos_s))
    ck = cache_k[0].reshape(bs, WINDOW, KV_WIDTH)
    cv = cache_v[0].reshape(bs, WINDOW, KV_WIDTH)
    oa_s, kwin_s, vwin_s = _attention_decode(proj_s, sink_vec, tables_s, qnw2, knw2, bd2, (ck, cv),
                                             bs // seqs_per_step, seqs_per_step, tdec)
    h0 = state_ssm[0].reshape(bs, SSD_WIDTH, D_STATE)
    os_s, conv_s, ssm_s = _ssd_decode(proj_s, dt_s, ssd_consts, (state_conv[0], h0), bs, tdec, 4)
    y_s = _outproj(xs2, proj_s, COL_G // (2 * D_MODEL), oa_s, os_s, w_ap, w_sp, w_o, 256).reshape(bs, tdec, D_MODEL)

    win = lambda a, n: a.reshape(1, n, WINDOW, N_KV_HEADS, HEAD_DIM)
    ssm = lambda a, n: a.reshape(1, n, SSD_HEADS, SSD_HEAD_DIM, D_STATE)
    return (y_p, y_s, win(kwin_p, bp), win(vwin_p, bp), conv_p[None], ssm(ssm_p, bp),
            win(kwin_s, bs), win(vwin_s, bs), conv_s[None], ssm(ssm_s, bs))
```

```python
import functools
import types

import jax
import jax.numpy as jnp
from jax import lax
from jax.experimental import pallas as pl
from jax.experimental.pallas import tpu as pltpu

F32 = jnp.float32
BF16 = jnp.bfloat16

D_MODEL = 1024
N_HEADS = 16
N_KV_HEADS = 4
HEAD_DIM = 64
KV_GROUP = N_HEADS // N_KV_HEADS
ATTN_WIDTH = N_HEADS * HEAD_DIM
KV_WIDTH = N_KV_HEADS * HEAD_DIM
WINDOW = 128
ROPE_DIM = HEAD_DIM // 4
ROPE_THETA = 500000.0
PAST_LEN = 16384
SSD_WIDTH = 2 * D_MODEL
SSD_HEAD_DIM = 64
SSD_HEADS = SSD_WIDTH // SSD_HEAD_DIM
SSD_GROUPS = 4
SSD_HPG = SSD_HEADS // SSD_GROUPS
D_STATE = 128
CONV_WIDTH = 4
CONV_DIM = SSD_WIDTH + 2 * SSD_GROUPS * D_STATE
SSD_CHUNK = 128
EPS = 1e-6

OFF_Q = ATTN_WIDTH
OFF_K = OFF_Q + KV_WIDTH
OFF_V = OFF_K + KV_WIDTH
OFF_ZA = OFF_V + ATTN_WIDTH
OFF_ZS = OFF_ZA + SSD_WIDTH
OFF_XBC = OFF_ZS + CONV_DIM
OFF_DT = OFF_XBC + SSD_HEADS

LANES = 128
SUBLANES = 8
VMEM_LIMIT_BYTES = 56 * 1024 * 1024

COL_ZS = 0
COL_G = COL_ZS + SSD_WIDTH
COL_Q = COL_G + 2 * D_MODEL
COL_ZA = COL_Q + ATTN_WIDTH
COL_XBC = COL_ZA + ATTN_WIDTH
COL_KV = COL_XBC + CONV_DIM
N_MAIN = COL_KV + 2 * KV_WIDTH
PROJ_TILE_N = 256
TRUNK_TILE = 2 * SSD_CHUNK
GROUP_WIDTH = SSD_WIDTH // SSD_GROUPS
HALF = LANES // 2
NEG_BIG = -1e30
LOG2E = 1.4426950408889634


def _sigmoid(x):
    return 0.5 * jnp.tanh(0.5 * x) + 0.5


def _silu(x):
    hx = 0.5 * x
    return hx * jnp.tanh(hx) + hx


def _split2(v):
    hi = v.astype(BF16)
    lo = (v - hi.astype(F32)).astype(BF16)
    return hi, lo


def _drain(gen):
    for _ in gen:
        pass


def _interleave(main, side, every):
    live = {"main": True, "side": True}

    def step(gen, key):
        if live[key]:
            try:
                next(gen)
            except StopIteration:
                live[key] = False

    while live["main"] or live["side"]:
        for _ in range(every):
            step(main, "main")
        step(side, "side")


def _inproj_gen(x_ref, nw_ref, w_ref, wdt_ref, store_cols, store_dt, h_scr):
    x = x_ref[...]
    ms = jnp.mean(x * x, axis=-1, keepdims=True)
    h_scr[...] = (x * lax.rsqrt(ms + EPS) * nw_ref[...]).astype(BF16)
    yield
    for j in range(N_MAIN // PROJ_TILE_N):
        sl = slice(j * PROJ_TILE_N, (j + 1) * PROJ_TILE_N)
        store_cols(j, jnp.dot(h_scr[...], w_ref[:, sl], preferred_element_type=F32))
        yield
    store_dt(jnp.dot(h_scr[...], wdt_ref[...], preferred_element_type=F32))
    yield


def _inproj_body(x_ref, nw_ref, w_ref, wdt_ref, out_ref, dt_ref, h_scr):
    def store_cols(j, val):
        out_ref[:, j * PROJ_TILE_N:(j + 1) * PROJ_TILE_N] = val.astype(out_ref.dtype)

    def store_dt(val):
        dt_ref[...] = val

    _drain(_inproj_gen(x_ref, nw_ref, w_ref, wdt_ref, store_cols, store_dt, h_scr))


def _inproj(x2d, norm_w, w_main, w_dt, tm, out_dtype):
    m = x2d.shape[0]
    return pl.pallas_call(
        _inproj_body,
        grid=(m // tm,),
        in_specs=[
            pl.BlockSpec((tm, D_MODEL), lambda i: (i, 0)),
            pl.BlockSpec((1, D_MODEL), lambda i: (0, 0)),
            pl.BlockSpec((D_MODEL, N_MAIN), lambda i: (0, 0), pipeline_mode=pl.Buffered(1)),
            pl.BlockSpec((D_MODEL, LANES), lambda i: (0, 0), pipeline_mode=pl.Buffered(1)),
        ],
        out_specs=[
            pl.BlockSpec((tm, N_MAIN), lambda i: (i, 0)),
            pl.BlockSpec((tm, LANES), lambda i: (i, 0)),
        ],
        out_shape=[
            jax.ShapeDtypeStruct((m, N_MAIN), out_dtype),
            jax.ShapeDtypeStruct((m, LANES), F32),
        ],
        scratch_shapes=[pltpu.VMEM((tm, D_MODEL), BF16)],
        compiler_params=pltpu.CompilerParams(
            dimension_semantics=("arbitrary",), vmem_limit_bytes=VMEM_LIMIT_BYTES),
        name="inproj",
    )(x2d, norm_w, w_main, w_dt)


def _outproj_body(x_ref, g_ref, oa_ref, os_ref, wa_ref, ws_ref, wo_ref, y_ref):
    pa = jnp.dot(oa_ref[...].astype(BF16), wa_ref[...], preferred_element_type=F32)
    ps = jnp.dot(os_ref[...].astype(BF16), ws_ref[...], preferred_element_type=F32)
    g = g_ref[...].astype(F32)
    merged = _sigmoid(g[:, :D_MODEL]) * pa + _sigmoid(g[:, D_MODEL:]) * ps
    y_ref[...] = x_ref[...] + jnp.dot(merged.astype(BF16), wo_ref[...], preferred_element_type=F32)


def _outproj(x2d, gates, gate_col_block, o_a, o_s, w_ap, w_sp, w_o, tm):
    m = x2d.shape[0]
    const = lambda i: (0, 0)
    return pl.pallas_call(
        _outproj_body,
        grid=(m // tm,),
        in_specs=[
            pl.BlockSpec((tm, D_MODEL), lambda i: (i, 0)),
            pl.BlockSpec((tm, 2 * D_MODEL), lambda i: (i, gate_col_block)),
            pl.BlockSpec((tm, ATTN_WIDTH), lambda i: (i, 0)),
            pl.BlockSpec((tm, SSD_WIDTH), lambda i: (i, 0)),
            pl.BlockSpec((ATTN_WIDTH, D_MODEL), const, pipeline_mode=pl.Buffered(1)),
            pl.BlockSpec((SSD_WIDTH, D_MODEL), const, pipeline_mode=pl.Buffered(1)),
            pl.BlockSpec((D_MODEL, D_MODEL), const, pipeline_mode=pl.Buffered(1)),
        ],
        out_specs=pl.BlockSpec((tm, D_MODEL), lambda i: (i, 0)),
        out_shape=jax.ShapeDtypeStruct((m, D_MODEL), F32),
        compiler_params=pltpu.CompilerParams(
            dimension_semantics=("arbitrary",), vmem_limit_bytes=VMEM_LIMIT_BYTES),
        name="outproj",
    )(x2d, gates, o_a, o_s, w_ap, w_sp, w_o)


def _norm_rope(xc, nw, bd2, cos_t, sin_lo, sin_hi):
    hi, lo = _split2(xc * xc)
    ms = jnp.dot(jnp.concatenate([hi, lo], axis=1), bd2, preferred_element_type=F32)
    xn = xc * lax.rsqrt(ms + EPS) * nw
    return xn * cos_t + pltpu.roll(xn, LANES - ROPE_DIM // 2, 1) * sin_lo + pltpu.roll(xn, ROPE_DIM // 2, 1) * sin_hi


def _dup_half(chunk, keep_low, lane_lo):
    rolled = pltpu.roll(chunk, HALF, 1)
    return jnp.where(lane_lo, chunk, rolled) if keep_low else jnp.where(lane_lo, rolled, chunk)


def _attn_reset(kdup, vdup):
    for g in range(N_KV_HEADS):
        kdup[0, g, WINDOW:2 * WINDOW, :] = jnp.zeros((WINDOW, LANES), BF16)
        vdup[0, g, WINDOW:2 * WINDOW, 0:LANES] = jnp.zeros((WINDOW, LANES), BF16)
    vdup[:, :, :, LANES:2 * LANES] = jnp.ones((vdup.shape[0], N_KV_HEADS, 2 * WINDOW, LANES), BF16)


def _attn_gen(io, sinks_ref, qnw_ref, knw_ref, bd2_ref, kdup, vdup, o_scr, *, seqs, t, decode, prev_valid,
              own_half):
    tp = max(t, 2 * SUBLANES)
    nkeys = 2 * WINDOW
    own_lo, prev_lo = own_half * WINDOW, (1 - own_half) * WINDOW
    bd2 = bd2_ref[...]
    cos_t, sin_lo, sin_hi = io.tables()
    lane_lo = lax.broadcasted_iota(jnp.int32, (1, LANES), 1) < HALF

    kv = io.kv().astype(F32)
    k_rot = [_norm_rope(kv[:, c * LANES:(c + 1) * LANES], knw_ref[...], bd2, cos_t, sin_lo, sin_hi)
             for c in range(KV_WIDTH // LANES)]
    v_raw = [kv[:, KV_WIDTH + c * LANES:KV_WIDTH + (c + 1) * LANES] for c in range(KV_WIDTH // LANES)]
    yield
    q_all = io.q().astype(F32)
    q_rot = []
    for c in range(ATTN_WIDTH // LANES):
        q_rot.append(_norm_rope(q_all[:, c * LANES:(c + 1) * LANES], qnw_ref[...], bd2, cos_t, sin_lo, sin_hi))
        if c % 2 == 1:
            yield

    if decode:
        vdup[:, :, :, LANES:2 * LANES] = jnp.ones((seqs, N_KV_HEADS, nkeys, LANES), BF16)

    own = (lax.broadcasted_iota(jnp.int32, (t, WINDOW), 1) <= lax.broadcasted_iota(jnp.int32, (t, WINDOW), 0))
    prev_bias = None if prev_valid is None else jnp.where(prev_valid, 0.0, NEG_BIG)

    def pad_rows(a):
        if tp == t:
            return a
        return jnp.concatenate([a, jnp.zeros((tp - t, a.shape[1]), a.dtype)], axis=0)

    for si in range(seqs):
        rs = slice(si * t, (si + 1) * t)
        for g in range(N_KV_HEADS):
            c, low = g // 2, (g % 2 == 0)
            kdup[si, g, own_lo:own_lo + tp, :] = pad_rows(_dup_half(k_rot[c][rs], low, lane_lo)).astype(BF16)
            vdup[si, g, own_lo:own_lo + tp, 0:LANES] = pad_rows(_dup_half(v_raw[c][rs], low, lane_lo)).astype(BF16)
            if decode:
                kdup[si, g, prev_lo:prev_lo + WINDOW, :] = _dup_half(
                    io.ck(si)[:, c * LANES:(c + 1) * LANES], low, lane_lo).astype(BF16)
                vdup[si, g, prev_lo:prev_lo + WINDOW, 0:LANES] = _dup_half(
                    io.cv(si)[:, c * LANES:(c + 1) * LANES], low, lane_lo).astype(BF16)
                kdup[si, g, own_lo + tp:own_lo + WINDOW, :] = jnp.zeros((WINDOW - tp, LANES), BF16)
                vdup[si, g, own_lo + tp:own_lo + WINDOW, 0:LANES] = jnp.zeros((WINDOW - tp, LANES), BF16)
        yield
        scores = []
        for g in range(N_KV_HEADS):
            q_stack = jnp.concatenate(
                [jnp.where(lane_lo if j % 2 == 0 else jnp.logical_not(lane_lo), q_rot[j // 2][rs], 0.0)
                 for j in range(KV_GROUP * g, KV_GROUP * (g + 1))], axis=0).astype(BF16)
            scores.append(lax.dot_general(q_stack, kdup[si, g], (((1,), (1,)), ((), ())),
                                          preferred_element_type=F32))
            yield
        probs, sink_terms = [], []
        for j in range(N_HEADS):
            s = scores[j // KV_GROUP]
            r0 = (j % KV_GROUP) * t
            s_prev = s[r0:r0 + t, prev_lo:prev_lo + WINDOW]
            if prev_bias is not None:
                s_prev = s_prev + prev_bias
            sj = jnp.where(own, s[r0:r0 + t, own_lo:own_lo + WINDOW], s_prev)
            sink = sinks_ref[j]
            mx = jnp.maximum(jnp.max(sj, axis=-1, keepdims=True), sink)
            p = jnp.exp(sj - mx)
            sink_terms.append(jnp.exp(sink - mx))
            halves = [jnp.where(own, 0.0, p), jnp.where(own, p, 0.0)]
            probs.append(jnp.concatenate(halves if own_half == 1 else halves[::-1], axis=1))
            if j % KV_GROUP == KV_GROUP - 1:
                yield
        for g in range(N_KV_HEADS):
            p_all = jnp.concatenate(probs[KV_GROUP * g:KV_GROUP * (g + 1)], axis=0).astype(BF16)
            o_aug = jnp.dot(p_all, vdup[si, g], preferred_element_type=F32)
            o = [o_aug[jj * t:(jj + 1) * t, 0:LANES]
                 / (o_aug[jj * t:(jj + 1) * t, LANES:2 * LANES] + sink_terms[KV_GROUP * g + jj])
                 for jj in range(KV_GROUP)]
            for half in range(2):
                o_scr[rs, (2 * g + half) * LANES:(2 * g + half + 1) * LANES] = jnp.where(
                    lane_lo, o[2 * half], o[2 * half + 1])
            yield

    io.oa(o_scr[...] * _silu(io.za().astype(F32)))
    yield

    k_new = jnp.concatenate(k_rot, axis=1)
    v_new = jnp.concatenate(v_raw, axis=1)
    if decode:
        for si in range(seqs):
            io.kwin(si, 0, WINDOW - t, io.ck(si)[t:WINDOW, :])
            io.vwin(si, 0, WINDOW - t, io.cv(si)[t:WINDOW, :])
            io.kwin(si, WINDOW - t, WINDOW, k_new[si * t:(si + 1) * t])
            io.vwin(si, WINDOW - t, WINDOW, v_new[si * t:(si + 1) * t])
    else:
        io.kwin(0, 0, WINDOW, k_new)
        io.vwin(0, 0, WINDOW, v_new)
    yield


def _attn_scratch(seqs, rows):
    return [
        pltpu.VMEM((seqs, N_KV_HEADS, 2 * WINDOW, LANES), BF16),
        pltpu.VMEM((seqs, N_KV_HEADS, 2 * WINDOW, 2 * LANES), BF16),
        pltpu.VMEM((rows, ATTN_WIDTH), F32),
    ]


def _attn_decode_body(sinks_ref, q_ref, kv_ref, za_ref, cos_ref, slo_ref, shi_ref, qnw_ref, knw_ref, bd2_ref,
                      ck_ref, cv_ref, oa_ref, kwin_ref, vwin_ref, kdup, vdup, o_scr, *, seqs, t):
    def store_oa(val):
        oa_ref[...] = val.astype(oa_ref.dtype)

    def store_kwin(si, lo, hi, val):
        kwin_ref[si, lo:hi, :] = val

    def store_vwin(si, lo, hi, val):
        vwin_ref[si, lo:hi, :] = val

    io = types.SimpleNamespace(
        q=lambda: q_ref[...], kv=lambda: kv_ref[...], za=lambda: za_ref[...],
        tables=lambda: (cos_ref[...], slo_ref[...], shi_ref[...]),
        oa=store_oa, kwin=store_kwin, vwin=store_vwin, ck=lambda si: ck_ref[si], cv=lambda si: cv_ref[si])
    _drain(_attn_gen(io, sinks_ref, qnw_ref, knw_ref, bd2_ref, kdup, vdup, o_scr,
                     seqs=seqs, t=t, decode=True, prev_valid=None, own_half=1))


def _attention_decode(proj, sinks, tables, qnw2, knw2, bd2, caches, nsteps, seqs, t):
    rows = seqs * t
    m = proj.shape[0]
    const = lambda s: (0, 0)
    in_specs = [
        pl.BlockSpec(memory_space=pltpu.SMEM),
        pl.BlockSpec((rows, ATTN_WIDTH), lambda s: (s, COL_Q // ATTN_WIDTH)),
        pl.BlockSpec((rows, 2 * KV_WIDTH), lambda s: (s, COL_KV // (2 * KV_WIDTH))),
        pl.BlockSpec((rows, ATTN_WIDTH), lambda s: (s, COL_ZA // ATTN_WIDTH)),
        pl.BlockSpec((rows, LANES), const),
        pl.BlockSpec((rows, LANES), const),
        pl.BlockSpec((rows, LANES), const),
        pl.BlockSpec((1, LANES), const),
        pl.BlockSpec((1, LANES), const),
        pl.BlockSpec((2 * LANES, LANES), const),
        pl.BlockSpec((seqs, WINDOW, KV_WIDTH), lambda s: (s, 0, 0)),
        pl.BlockSpec((seqs, WINDOW, KV_WIDTH), lambda s: (s, 0, 0)),
    ]
    return pl.pallas_call(
        functools.partial(_attn_decode_body, seqs=seqs, t=t),
        grid=(nsteps,),
        in_specs=in_specs,
        out_specs=[
            pl.BlockSpec((rows, ATTN_WIDTH), lambda s: (s, 0)),
            pl.BlockSpec((seqs, WINDOW, KV_WIDTH), lambda s: (s, 0, 0)),
            pl.BlockSpec((seqs, WINDOW, KV_WIDTH), lambda s: (s, 0, 0)),
        ],
        out_shape=[
            jax.ShapeDtypeStruct((m, ATTN_WIDTH), F32),
            jax.ShapeDtypeStruct((nsteps * seqs, WINDOW, KV_WIDTH), F32),
            jax.ShapeDtypeStruct((nsteps * seqs, WINDOW, KV_WIDTH), F32),
        ],
        scratch_shapes=_attn_scratch(seqs, rows),
        compiler_params=pltpu.CompilerParams(
            dimension_semantics=("arbitrary",), vmem_limit_bytes=VMEM_LIMIT_BYTES),
        name="attn_decode",
    )(sinks, proj, proj, proj, *tables, qnw2, knw2, bd2, *caches)


def _ssd_gen(io, cw_ref, cb_ref, dtb_ref, alog_ref, dsk_ref, nw_ref, e2_ref, tri_ref,
             xp, xs_scr, bc_scr, acs_scr, acst_scr, dtt_scr, ht, y_scr, *, q, decode):
    lc = SSD_CHUNK
    tail = CONV_WIDTH - 1
    base = SUBLANES

    slab = 4 * LANES
    for cs in range(CONV_DIM // slab):
        cl = slice(cs * slab, (cs + 1) * slab)
        if decode:
            acc = cb_ref[:, cl] + cw_ref[0:1, cl] * xp[base - tail:base - tail + q, cl]
            for j in range(1, CONV_WIDTH):
                acc = acc + cw_ref[j:j + 1, cl] * xp[base - tail + j:base - tail + j + q, cl]
        else:
            xe = jnp.concatenate([xp[:, cl], io.xbc(cl).astype(F32)], axis=0)
            acc = cw_ref[0:1, cl] * xe
            for j in range(1, CONV_WIDTH):
                acc = pltpu.roll(acc, 1, 0) + cw_ref[j:j + 1, cl] * xe
            acc = acc[SUBLANES:] + cb_ref[:, cl]
        act = _silu(acc)
        if cs < SSD_WIDTH // slab:
            xs_scr[0:q, cl] = act
        else:
            bc_scr[0:q, cs * slab - SSD_WIDTH:(cs + 1) * slab - SSD_WIDTH] = act
        yield
    if not decode:
        xp[...] = io.xbc_last_rows().astype(F32)[SUBLANES:2 * SUBLANES]

    dt_in = io.dt()
    if q < lc:
        dt_in = jnp.concatenate([dt_in, jnp.zeros((lc - q, LANES), F32)], axis=0)
    xdt = dt_in + dtb_ref[...]
    dt = jnp.maximum(xdt, 0.0) + jnp.log1p(jnp.exp(-jnp.abs(xdt)))
    if q < lc:
        dt = jnp.where(lax.broadcasted_iota(jnp.int32, (lc, LANES), 0) < q, dt, 0.0)
    a = dt * (-jnp.exp(alog_ref[...]))
    a_hi = a.astype(BF16)
    a_r1 = a - a_hi.astype(F32)
    a_mid = a_r1.astype(BF16)
    a_lo = (a_r1 - a_mid.astype(F32)).astype(BF16)
    acs3 = jnp.dot(tri_ref[...], jnp.concatenate([a_hi, a_mid, a_lo], axis=1), preferred_element_type=F32)
    acs = acs3[:, 0:LANES] + acs3[:, LANES:2 * LANES] + acs3[:, 2 * LANES:3 * LANES]
    acs2 = acs * LOG2E
    acs_scr[...] = acs2
    acst_scr[...] = (acs2 - jnp.log2(dt)).T
    yield

    def expand(v):
        hi, lo = _split2(v)
        return jnp.dot(jnp.concatenate([hi, lo], axis=1), e2_ref[...], preferred_element_type=F32)

    both = expand(jnp.concatenate([jnp.exp(acs[0:q]), (jnp.exp(acs[lc - 1:lc, :] - acs) * dt)[0:q]], axis=0))
    e_off, w_state = both[0:q], both[q:2 * q]
    chunk_decay = e_off[q - 1:q, :]
    yield

    causal = (lax.broadcasted_iota(jnp.int32, (q, lc), 0) >= lax.broadcasted_iota(jnp.int32, (q, lc), 1))
    lane_lo = lax.broadcasted_iota(jnp.int32, (1, LANES), 1) < HALF
    nbc = SSD_GROUPS * D_STATE
    pairs_per_group = SSD_HPG // 2
    gls = [slice(g * GROUP_WIDTH, (g + 1) * GROUP_WIDTH) for g in range(SSD_GROUPS)]
    b_gs = [bc_scr[:, g * D_STATE:(g + 1) * D_STATE] for g in range(SSD_GROUPS)]
    c_gs = [bc_scr[0:q, nbc + g * D_STATE:nbc + (g + 1) * D_STATE].astype(BF16) for g in range(SSD_GROUPS)]
    cbs = [lax.dot_general(c_gs[g], b_gs[g].astype(BF16), (((1,), (1,)), ((), ())), preferred_element_type=F32)
           for g in range(SSD_GROUPS)]
    yield
    y_offs = []
    for g in range(SSD_GROUPS):
        y_offs.append(jnp.dot(c_gs[g], ht[g].astype(BF16), preferred_element_type=F32) * e_off[0:q, gls[g]])
        yield
    for g in range(SSD_GROUPS):
        xw = xs_scr[0:q, gls[g]] * w_state[:, gls[g]]
        if q < lc:
            xw = jnp.concatenate([xw, jnp.zeros((lc - q, GROUP_WIDTH), F32)], axis=0)
        xw = xw.astype(BF16)
        ht[g] = ht[g] * chunk_decay[:, gls[g]] + jnp.dot(b_gs[g].T.astype(BF16), xw, preferred_element_type=F32)
        yield
    w_pairs = []
    for pair in range(SSD_HEADS // 2):
        w_parts = []
        for h in (2 * pair, 2 * pair + 1):
            seg2 = acs_scr[0:q, h:h + 1] - acst_scr[h:h + 1, :]
            decay = jnp.exp2(jnp.where(causal, seg2, -jnp.inf))
            w_parts.append(cbs[pair // pairs_per_group] * decay)
        w_pairs.append(jnp.concatenate(w_parts, axis=1).astype(BF16))
        if pair % 2 == 1:
            yield
    y_diags = []
    for pair in range(SSD_HEADS // 2):
        x_pair = xs_scr[:, pair * LANES:(pair + 1) * LANES]
        x_bd = jnp.concatenate([jnp.where(lane_lo, x_pair, 0.0), jnp.where(lane_lo, 0.0, x_pair)],
                               axis=0).astype(BF16)
        y_diags.append(jnp.dot(w_pairs[pair], x_bd, preferred_element_type=F32))
        if pair % 4 == 3:
            yield
    for pair in range(SSD_HEADS // 2):
        g, pr = divmod(pair, pairs_per_group)
        pl_ = slice(pair * LANES, (pair + 1) * LANES)
        y_pair = y_diags[pair] + y_offs[g][:, pr * LANES:(pr + 1) * LANES] + dsk_ref[:, pl_] * xs_scr[0:q, pl_]
        y_scr[:, pl_] = y_pair * _silu(io.zs(pl_).astype(F32))
        if pair % 4 == 3:
            yield
    for g in range(SSD_GROUPS):
        yg = y_scr[:, gls[g]]
        ms = jnp.mean(yg * yg, axis=-1, keepdims=True)
        io.os(gls[g], yg * lax.rsqrt(ms + EPS) * nw_ref[:, gls[g]])
        yield


def _ssd_write_state(hout_ref, ht):
    for g in range(SSD_GROUPS):
        hout_ref[0, g * GROUP_WIDTH:(g + 1) * GROUP_WIDTH, :] = ht[g].T


def _ssd_scratch(q, staged_rows):
    return [
        pltpu.VMEM((SUBLANES + staged_rows, CONV_DIM), F32),
        pltpu.VMEM((SSD_CHUNK, SSD_WIDTH), F32),
        pltpu.VMEM((SSD_CHUNK, 2 * SSD_GROUPS * D_STATE), F32),
        pltpu.VMEM((SSD_CHUNK, LANES), F32),
        pltpu.VMEM((LANES, SSD_CHUNK), F32),
        pltpu.VMEM((LANES, SSD_CHUNK), F32),
        pltpu.VMEM((SSD_GROUPS, D_STATE, GROUP_WIDTH), F32),
        pltpu.VMEM((q, SSD_WIDTH), F32),
    ]


def _ssd_decode_body(xbc_ref, zs_ref, dt_ref, cw_ref, cb_ref, dtb_ref, alog_ref, dsk_ref, nw_ref, e2_ref, tri_ref,
                     cst_ref, h0_ref, os_ref, cout_ref, hout_ref, *scratch, q, seqs):
    tail = CONV_WIDTH - 1
    base = SUBLANES
    per_seq = len(scratch) // seqs
    gens = []
    for si in range(seqs):
        xp, xs_scr, bc_scr, acs_scr, acst_scr, dtt_scr, ht, y_scr = scratch[si * per_seq:(si + 1) * per_seq]
        rows = slice(si * q, (si + 1) * q)

        @pl.when(pl.program_id(0) == 0)
        def _(xs_scr=xs_scr, bc_scr=bc_scr):
            xs_scr[q:SSD_CHUNK, :] = jnp.zeros((SSD_CHUNK - q, SSD_WIDTH), F32)
            bc_scr[q:SSD_CHUNK, :] = jnp.zeros((SSD_CHUNK - q, 2 * SSD_GROUPS * D_STATE), F32)

        xp[base - tail:base, :] = cst_ref[si]
        xp[base:base + q, :] = xbc_ref[rows, :].astype(F32)
        for g in range(SSD_GROUPS):
            ht[g] = h0_ref[si, g * GROUP_WIDTH:(g + 1) * GROUP_WIDTH, :].T

        def store_os(cols, val, rows=rows):
            os_ref[rows, cols] = val.astype(os_ref.dtype)

        io = types.SimpleNamespace(zs=lambda cols, rows=rows: zs_ref[rows, cols], dt=lambda rows=rows: dt_ref[rows, :],
                                   os=store_os)
        gens.append(_ssd_gen(io, cw_ref, cb_ref, dtb_ref, alog_ref, dsk_ref, nw_ref, e2_ref, tri_ref,
                             xp, xs_scr, bc_scr, acs_scr, acst_scr, dtt_scr, ht, y_scr, q=q, decode=True))
    while gens:
        for gen in list(gens):
            try:
                next(gen)
            except StopIteration:
                gens.remove(gen)
    for si in range(seqs):
        xp, ht = scratch[si * per_seq], scratch[si * per_seq + 6]
        cout_ref[si] = xp[base + q - tail:base + q, :]
        for g in range(SSD_GROUPS):
            hout_ref[si, g * GROUP_WIDTH:(g + 1) * GROUP_WIDTH, :] = ht[g].T


def _ssd_const_specs(const):
    return [
        pl.BlockSpec((CONV_WIDTH, CONV_DIM), const),
        pl.BlockSpec((1, CONV_DIM), const),
        pl.BlockSpec((1, LANES), const),
        pl.BlockSpec((1, LANES), const),
        pl.BlockSpec((1, SSD_WIDTH), const),
        pl.BlockSpec((1, SSD_WIDTH), const),
        pl.BlockSpec((2 * LANES, SSD_WIDTH), const),
        pl.BlockSpec((SSD_CHUNK, SSD_CHUNK), const),
    ]


def _ssd_decode(proj, dt_raw, consts, states, nseq, q, seqs):
    m = proj.shape[0]
    rows = seqs * q
    const = lambda s: (0, 0)
    in_specs = [
        pl.BlockSpec((rows, CONV_DIM), lambda s: (s, COL_XBC // CONV_DIM)),
        pl.BlockSpec((rows, SSD_WIDTH), lambda s: (s, COL_ZS // SSD_WIDTH)),
        pl.BlockSpec((rows, LANES), lambda s: (s, 0)),
        *_ssd_const_specs(const),
        pl.BlockSpec((seqs, CONV_WIDTH - 1, CONV_DIM), lambda s: (s, 0, 0)),
        pl.BlockSpec((seqs, SSD_WIDTH, D_STATE), lambda s: (s, 0, 0)),
    ]
    return pl.pallas_call(
        functools.partial(_ssd_decode_body, q=q, seqs=seqs),
        grid=(nseq // seqs,),
        in_specs=in_specs,
        out_specs=[
            pl.BlockSpec((rows, SSD_WIDTH), lambda s: (s, 0)),
            pl.BlockSpec((seqs, CONV_WIDTH - 1, CONV_DIM), lambda s: (s, 0, 0)),
            pl.BlockSpec((seqs, SSD_WIDTH, D_STATE), lambda s: (s, 0, 0)),
        ],
        out_shape=[
            jax.ShapeDtypeStruct((m, SSD_WIDTH), F32),
            jax.ShapeDtypeStruct((nseq, CONV_WIDTH - 1, CONV_DIM), F32),
            jax.ShapeDtypeStruct((nseq, SSD_WIDTH, D_STATE), F32),
        ],
        scratch_shapes=_ssd_scratch(q, q) * seqs,
        compiler_params=pltpu.CompilerParams(
            dimension_semantics=("arbitrary",), vmem_limit_bytes=VMEM_LIMIT_BYTES),
        name="ssd_decode",
    )(proj, proj, dt_raw, *consts, *states)


def _trunk_body(sinks_ref, x_ref, nw_ref, w_ref, wdt_ref, cos_ref, slo_ref, shi_ref, qnw_ref, knw_ref, bd2_ref,
                cw_ref, cb_ref, dtb_ref, alog_ref, dsk_ref, snw_ref, e2_ref, tri_ref,
                g_ref, oa_ref, os_ref, kwin_ref, vwin_ref, cout_ref, hout_ref,
                proj_a, proj_b, dt_a, dt_b, h_scr, kdup, vdup, o_scr,
                xp, xs_scr, bc_scr, acs_scr, acst_scr, dtt_scr, ht, y_scr, *, tiles_per_seq):
    j = pl.program_id(0)
    tile_in_seq = (j + tiles_per_seq - 1) % tiles_per_seq
    seq_start = tile_in_seq == 0
    chunks = TRUNK_TILE // SSD_CHUNK
    assert chunks % 2 == 0, "chunk parity picks the key/value scratch half across tiles"
    g_tiles = range(COL_G // PROJ_TILE_N, COL_Q // PROJ_TILE_N)

    @pl.when(j == 0)
    def _():
        proj_b[...] = jnp.zeros(proj_b.shape, proj_b.dtype)
        dt_b[...] = jnp.zeros(dt_b.shape, F32)

    @pl.when(seq_start | (j == 0))
    def _():
        _attn_reset(kdup, vdup)
        xp[...] = jnp.zeros(xp.shape, F32)
        ht[...] = jnp.zeros(ht.shape, F32)

    def step(wbuf, wdt, rbuf, rdt):
        def store_cols(jn, val):
            if jn in g_tiles:
                g_ref[:, (jn - g_tiles[0]) * PROJ_TILE_N:(jn - g_tiles[0] + 1) * PROJ_TILE_N] = val.astype(g_ref.dtype)
            else:
                wbuf[:, jn * PROJ_TILE_N:(jn + 1) * PROJ_TILE_N] = val.astype(wbuf.dtype)

        def store_dt(val):
            wdt[...] = val

        producer = _inproj_gen(x_ref, nw_ref, w_ref, wdt_ref, store_cols, store_dt, h_scr)

        def consumer():
            for c in range(chunks):
                rows = slice(c * SSD_CHUNK, (c + 1) * SSD_CHUNK)

                def store_oa(val, rows=rows):
                    oa_ref[rows, :] = val.astype(oa_ref.dtype)

                def store_kwin(si, lo, hi, val):
                    kwin_ref[0, lo:hi, :] = val

                def store_vwin(si, lo, hi, val):
                    vwin_ref[0, lo:hi, :] = val

                def store_os(cols, val, rows=rows):
                    os_ref[rows, cols] = val.astype(os_ref.dtype)

                attn_io = types.SimpleNamespace(
                    q=lambda rows=rows: rbuf[rows, COL_Q:COL_Q + ATTN_WIDTH],
                    kv=lambda rows=rows: rbuf[rows, COL_KV:COL_KV + 2 * KV_WIDTH],
                    za=lambda rows=rows: rbuf[rows, COL_ZA:COL_ZA + ATTN_WIDTH],
                    tables=lambda rows=rows: (cos_ref[rows, :], slo_ref[rows, :], shi_ref[rows, :]),
                    oa=store_oa, kwin=store_kwin, vwin=store_vwin)
                ssd_io = types.SimpleNamespace(
                    xbc=lambda cols, rows=rows: rbuf[rows, COL_XBC + cols.start:COL_XBC + cols.stop],
                    xbc_last_rows=lambda c=c: rbuf[(c + 1) * SSD_CHUNK - 2 * SUBLANES:(c + 1) * SSD_CHUNK,
                                                   COL_XBC:COL_XBC + CONV_DIM],
                    zs=lambda cols, rows=rows: rbuf[rows, COL_ZS + cols.start:COL_ZS + cols.stop],
                    dt=lambda rows=rows: rdt[rows, :], os=store_os)
                prev_valid = (tile_in_seq > 0) if c == 0 else True
                attn = _attn_gen(attn_io, sinks_ref, qnw_ref, knw_ref, bd2_ref, kdup, vdup, o_scr,
                                 seqs=1, t=WINDOW, decode=False, prev_valid=prev_valid, own_half=c % 2)
                ssd = _ssd_gen(ssd_io, cw_ref, cb_ref, dtb_ref, alog_ref, dsk_ref, snw_ref, e2_ref, tri_ref,
                               xp, xs_scr, bc_scr, acs_scr, acst_scr, dtt_scr, ht, y_scr, q=SSD_CHUNK, decode=False)
                live = [attn, ssd]
                while live:
                    for gen in list(live):
                        try:
                            next(gen)
                        except StopIteration:
                            live.remove(gen)
                        yield

        _interleave(consumer(), producer, every=3)

    pl.when(j % 2 == 0)(lambda: step(proj_a, dt_a, proj_b, dt_b))
    pl.when(j % 2 == 1)(lambda: step(proj_b, dt_b, proj_a, dt_a))

    @pl.when((tile_in_seq == tiles_per_seq - 1) & (j > 0))
    def _():
        cout_ref[0] = xp[SUBLANES - (CONV_WIDTH - 1):SUBLANES, :]
        _ssd_write_state(hout_ref, ht)


def _trunk(x2d, sinks, nw, w_main, w_dt, tables, qnw2, knw2, bd2, ssd_consts, nseq, seq):
    m = x2d.shape[0]
    tiles_per_seq = seq // TRUNK_TILE
    ntiles = m // TRUNK_TILE
    produced = lambda j: jnp.minimum(j, ntiles - 1)
    consumed = lambda j: jnp.maximum(j - 1, 0)
    const = lambda j: (0, 0)
    tab_map = lambda j: ((j + tiles_per_seq - 1) % tiles_per_seq, 0)
    seq_map = lambda j: (consumed(j) // tiles_per_seq, 0, 0)
    in_specs = [
        pl.BlockSpec(memory_space=pltpu.SMEM),
        pl.BlockSpec((TRUNK_TILE, D_MODEL), lambda j: (produced(j), 0)),
        pl.BlockSpec((1, D_MODEL), const),
        pl.BlockSpec((D_MODEL, N_MAIN), const, pipeline_mode=pl.Buffered(1)),
        pl.BlockSpec((D_MODEL, LANES), const, pipeline_mode=pl.Buffered(1)),
        pl.BlockSpec((TRUNK_TILE, LANES), tab_map),
        pl.BlockSpec((TRUNK_TILE, LANES), tab_map),
        pl.BlockSpec((TRUNK_TILE, LANES), tab_map),
        pl.BlockSpec((1, LANES), const),
        pl.BlockSpec((1, LANES), const),
        pl.BlockSpec((2 * LANES, LANES), const),
        *_ssd_const_specs(const),
    ]
    return pl.pallas_call(
        functools.partial(_trunk_body, tiles_per_seq=tiles_per_seq),
        grid=(ntiles + 1,),
        in_specs=in_specs,
        out_specs=[
            pl.BlockSpec((TRUNK_TILE, 2 * D_MODEL), lambda j: (produced(j), 0)),
            pl.BlockSpec((TRUNK_TILE, ATTN_WIDTH), lambda j: (consumed(j), 0)),
            pl.BlockSpec((TRUNK_TILE, SSD_WIDTH), lambda j: (consumed(j), 0)),
            pl.BlockSpec((1, WINDOW, KV_WIDTH), seq_map),
            pl.BlockSpec((1, WINDOW, KV_WIDTH), seq_map),
            pl.BlockSpec((1, CONV_WIDTH - 1, CONV_DIM), seq_map),
            pl.BlockSpec((1, SSD_WIDTH, D_STATE), seq_map),
        ],
        out_shape=[
            jax.ShapeDtypeStruct((m, 2 * D_MODEL), BF16),
            jax.ShapeDtypeStruct((m, ATTN_WIDTH), BF16),
            jax.ShapeDtypeStruct((m, SSD_WIDTH), BF16),
            jax.ShapeDtypeStruct((nseq, WINDOW, KV_WIDTH), F32),
            jax.ShapeDtypeStruct((nseq, WINDOW, KV_WIDTH), F32),
            jax.ShapeDtypeStruct((nseq, CONV_WIDTH - 1, CONV_DIM), F32),
            jax.ShapeDtypeStruct((nseq, SSD_WIDTH, D_STATE), F32),
        ],
        scratch_shapes=[
            pltpu.VMEM((TRUNK_TILE, N_MAIN), BF16),
            pltpu.VMEM((TRUNK_TILE, N_MAIN), BF16),
            pltpu.VMEM((TRUNK_TILE, LANES), F32),
            pltpu.VMEM((TRUNK_TILE, LANES), F32),
            pltpu.VMEM((TRUNK_TILE, D_MODEL), BF16),
            *_attn_scratch(1, WINDOW),
            *_ssd_scratch(SSD_CHUNK, 0),
        ],
        compiler_params=pltpu.CompilerParams(
            dimension_semantics=("arbitrary",), vmem_limit_bytes=VMEM_LIMIT_BYTES),
        name="prompt_trunk",
    )(sinks, x2d, nw, w_main, w_dt, *tables, qnw2, knw2, bd2, *ssd_consts)


def _rope_tables(pos):
    half = ROPE_DIM // 2
    inv_freq = jnp.power(ROPE_THETA, -jnp.arange(half, dtype=F32) * (2.0 / ROPE_DIM))
    ang = pos.astype(F32)[:, None] * inv_freq[None, :]
    cos, sin = jnp.cos(ang), jnp.sin(ang)
    n = pos.shape[0]
    rest = HEAD_DIM - ROPE_DIM
    z_half, z_rest = jnp.zeros((n, half), F32), jnp.zeros((n, rest), F32)
    cos_t = jnp.concatenate([cos, cos, jnp.ones((n, rest), F32)], axis=1)
    sin_lo = jnp.concatenate([-sin, z_half, z_rest], axis=1)
    sin_hi = jnp.concatenate([z_half, sin, z_rest], axis=1)
    return tuple(jnp.tile(tb, (1, LANES // HEAD_DIM)) for tb in (cos_t, sin_lo, sin_hi))


def _pad_lanes(v):
    return jnp.pad(v.reshape(1, -1).astype(F32), ((0, 0), (0, LANES - v.shape[-1])))


def kernel(x_prompt, x_sample, cache_k, cache_v, state_conv, state_ssm, norm_w, w_in, q_norm_w, k_norm_w, sinks,
           conv_w, conv_b, dt_bias, A_log, D_skip, ssd_norm_w, w_attn_proj, w_ssd_proj, w_out):
    depth = norm_w.shape[0]
    assert depth == 1, "single-layer trunk"
    bp, seq = x_prompt.shape[:2]
    bs, tdec = x_sample.shape[:2]
    assert seq % TRUNK_TILE == 0 and tdec == SUBLANES and cache_k.shape[2] == WINDOW

    w = w_in[0]
    w_main = jnp.concatenate([
        w[:, OFF_ZA:OFF_ZS], w[:, OFF_DT:], w[:, :OFF_Q], w[:, OFF_V:OFF_ZA], w[:, OFF_ZS:OFF_XBC], w[:, OFF_Q:OFF_V],
    ], axis=1).astype(BF16)
    w_dt = jnp.pad(w[:, OFF_XBC:OFF_DT], ((0, 0), (0, LANES - SSD_HEADS))).astype(BF16)
    nw = norm_w[0].reshape(1, D_MODEL)
    w_ap, w_sp, w_o = w_attn_proj[0].astype(BF16), w_ssd_proj[0].astype(BF16), w_out[0].astype(BF16)

    lane = jnp.arange(LANES)
    bd = (lane[:, None] // HEAD_DIM == lane[None, :] // HEAD_DIM).astype(F32) / HEAD_DIM
    bd2 = jnp.concatenate([bd, bd], axis=0).astype(BF16)
    head_of_lane = jnp.arange(SSD_WIDTH) // SSD_HEAD_DIM
    e1 = (lane[:, None] == head_of_lane[None, :]).astype(BF16)
    e2 = jnp.concatenate([e1, e1], axis=0)
    tri = (lane[:, None] >= lane[None, :]).astype(BF16)
    qnw2 = jnp.tile(q_norm_w[0].reshape(1, HEAD_DIM), (1, LANES // HEAD_DIM)) * (HEAD_DIM ** -0.5)
    knw2 = jnp.tile(k_norm_w[0].reshape(1, HEAD_DIM), (1, LANES // HEAD_DIM))
    ssd_consts = (conv_w[0], conv_b[0].reshape(1, CONV_DIM), _pad_lanes(dt_bias[0]), _pad_lanes(A_log[0]),
                  jnp.repeat(D_skip[0], SSD_HEAD_DIM).reshape(1, SSD_WIDTH), ssd_norm_w[0].reshape(1, SSD_WIDTH),
                  e2, tri)
    sink_vec = sinks[0].astype(F32)

    xp2 = x_prompt.reshape(bp * seq, D_MODEL)
    tables_p = _rope_tables(jnp.arange(seq, dtype=jnp.int32))
    gates_p, oa_p, os_p, kwin_p, vwin_p, conv_p, ssm_p = _trunk(
        xp2, sink_vec, nw, w_main, w_dt, tables_p, qnw2, knw2, bd2, ssd_consts, bp, seq)
    y_p = _outproj(xp2, gates_p, 0, oa_p, os_p, w_ap, w_sp, w_o, 1024).reshape(bp, seq, D_MODEL)

    xs2 = x_sample.reshape(bs * tdec, D_MODEL)
    seqs_per_step = 8
    proj_s, dt_s = _inproj(xs2, nw, w_main, w_dt, 256, F32)
    pos_s = PAST_LEN + jnp.arange(tdec, dtype=jnp.int32)
    tables_s = tuple(jnp.tile(tb, (seqs_per_step, 1)) for tb in _rope_tables(pos_s))
    ck = cache_k[0].reshape(bs, WINDOW, KV_WIDTH)
    cv = cache_v[0].reshape(bs, WINDOW, KV_WIDTH)
    oa_s, kwin_s, vwin_s = _attention_decode(proj_s, sink_vec, tables_s, qnw2, knw2, bd2, (ck, cv),
                                             bs // seqs_per_step, seqs_per_step, tdec)
    h0 = state_ssm[0].reshape(bs, SSD_WIDTH, D_STATE)
    os_s, conv_s, ssm_s = _ssd_decode(proj_s, dt_s, ssd_consts, (state_conv[0], h0), bs, tdec, 4)
    y_s = _outproj(xs2, proj_s, COL_G // (2 * D_MODEL), oa_s, os_s, w_ap, w_sp, w_o, 256).reshape(bs, tdec, D_MODEL)

    win = lambda a, n: a.reshape(1, n, WINDOW, N_KV_HEADS, HEAD_DIM)
    ssm = lambda a, n: a.reshape(1, n, SSD_HEADS, SSD_HEAD_DIM, D_STATE)
    return (y_p, y_s, win(kwin_p, bp), win(vwin_p, bp), conv_p[None], ssm(ssm_p, bp),
            win(kwin_s, bs), win(vwin_s, bs), conv_s[None], ssm(ssm_s, bs))
```
